```python
import jax, jax.numpy as jnp
from jax import lax
import numpy as np

D_MODEL = 2048
BATCH = 8
SEQ = 4096
DEPTH = 4
DEC_BATCH = 4
DEC_SEQ = 8192
PAST_LEN = 128

N_HEADS = 16
N_KV_HEADS = 4
HEAD_DIM = 128
GQA_GROUP = N_HEADS // N_KV_HEADS
QKV_DIM = (N_HEADS + 2 * N_KV_HEADS) * HEAD_DIM
ATTN_DIM = N_HEADS * HEAD_DIM
ROPE_THETA = 10000.0
AXIS_DIM = HEAD_DIM // 2
AXIS_FREQS = AXIS_DIM // 2
GRID_W = 64
Q_BLOCK = 128
CONV_K = 31
FFN_CONV_K = 3
D_FF = 5632
EPS = 1e-6
N_MIXERS = 2
N_ATTN_LAYERS = (DEPTH + 1) // 2
N_CONV_LAYERS = DEPTH // 2

kernel_name = "hybrid_gqa_conformer_convffn_encoder"


def rms_norm(x, gain):
    xf = x.astype(jnp.float32)
    y = xf * lax.rsqrt(jnp.mean(xf * xf, axis=-1, keepdims=True) + EPS)
    return (y * gain.astype(jnp.float32)).astype(x.dtype)


def layer_norm(x, gain, bias):
    xf = x.astype(jnp.float32)
    mu = jnp.mean(xf, axis=-1, keepdims=True)
    var = jnp.mean(jnp.square(xf - mu), axis=-1, keepdims=True)
    y = (xf - mu) * lax.rsqrt(var + EPS)
    return (y * gain.astype(jnp.float32) + bias.astype(jnp.float32)).astype(x.dtype)


def depthwise_conv(x, w, b):
    k = w.shape[0]
    y = lax.conv_general_dilated(
        x, w.astype(x.dtype)[:, None, :], window_strides=(1,),
        padding=((k // 2, k // 2),), dimension_numbers=('NWC', 'WIO', 'NWC'),
        feature_group_count=x.shape[-1])
    return y + b.astype(x.dtype)


def axial_rope_tables(seq_len, dtype):
    rows = seq_len // GRID_W
    row_pos = jnp.repeat(jnp.arange(rows, dtype=jnp.float32), GRID_W)
    col_pos = jnp.tile(jnp.arange(GRID_W, dtype=jnp.float32), rows)
    freqs = ROPE_THETA ** (-jnp.arange(AXIS_FREQS, dtype=jnp.float32) * 2.0 / AXIS_DIM)
    ang_r = row_pos[:, None] * freqs[None, :]
    ang_c = col_pos[:, None] * freqs[None, :]
    return tuple(t.astype(dtype)[None, :, None, :] for t in
                 (jnp.cos(ang_r), jnp.sin(ang_r), jnp.cos(ang_c), jnp.sin(ang_c)))


def apply_axial_rope(x, tables):
    cr, sr, cc, sc = tables
    x1, x2, x3, x4 = jnp.split(x, 4, axis=-1)
    return jnp.concatenate([x1 * cr - x2 * sr, x2 * cr + x1 * sr,
                            x3 * cc - x4 * sc, x4 * cc + x3 * sc], axis=-1)


def gqa_attention_mixer(h, w_qkv, q_gain, k_gain, w_o):
    b, s, _ = h.shape
    qkv = h @ w_qkv.astype(h.dtype)
    q, k, v = jnp.split(qkv, [ATTN_DIM, ATTN_DIM + N_KV_HEADS * HEAD_DIM], axis=-1)
    q = rms_norm(q.reshape(b, s, N_HEADS, HEAD_DIM), q_gain)
    k = rms_norm(k.reshape(b, s, N_KV_HEADS, HEAD_DIM), k_gain)
    v = v.reshape(b, s, N_KV_HEADS, HEAD_DIM)
    tables = axial_rope_tables(s, h.dtype)
    q = apply_axial_rope(q, tables)
    k = apply_axial_rope(k, tables)
    n_blk = s // Q_BLOCK
    qb = q.reshape(b, n_blk, Q_BLOCK, N_KV_HEADS, GQA_GROUP, HEAD_DIM).transpose(1, 0, 3, 4, 2, 5)
    kt = k.transpose(0, 2, 1, 3)
    vt = v.transpose(0, 2, 1, 3)
    scale = HEAD_DIM ** -0.5

    def one_block(q_blk):
        sc = jnp.einsum('bkgqd,bksd->bkgqs', q_blk, kt,
                        preferred_element_type=jnp.float32) * scale
        p = jax.nn.softmax(sc, axis=-1).astype(vt.dtype)
        return jnp.einsum('bkgqs,bksd->bkgqd', p, vt)

    o = lax.map(one_block, qb)
    o = o.transpose(1, 0, 4, 2, 3, 5).reshape(b, s, ATTN_DIM)
    return o @ w_o.astype(h.dtype)


def conformer_conv_mixer(h, w_pw1, b_pw1, w_dw, b_dw, ln_g, ln_b, w_pw2, b_pw2):
    u = h @ w_pw1.astype(h.dtype) + b_pw1.astype(h.dtype)
    a, g = jnp.split(u, 2, axis=-1)
    u = a * jax.nn.sigmoid(g)
    u = depthwise_conv(u, w_dw, b_dw)
    u = jax.nn.silu(layer_norm(u, ln_g, ln_b))
    return u @ w_pw2.astype(h.dtype) + b_pw2.astype(h.dtype)


def conv_gated_mlp(h, w_up, w_dw, b_dw, w_down):
    u = h @ w_up.astype(h.dtype)
    u = depthwise_conv(u, w_dw, b_dw)
    g, v = jnp.split(u, 2, axis=-1)
    return (jax.nn.silu(g) * v) @ w_down.astype(h.dtype)


def run_trunk(x, attn_norm, attn_w_qkv, attn_q_norm, attn_k_norm, attn_w_o,
              conv_norm, conv_w_pw1, conv_b_pw1, conv_w_dw, conv_b_dw,
              conv_ln_g, conv_ln_b, conv_w_pw2, conv_b_pw2,
              ffn_norm, ffn_w_up, ffn_w_dw, ffn_b_dw, ffn_w_down):
    for i in range(DEPTH):
        j = i // N_MIXERS
        if i % N_MIXERS == 0:
            x = x + gqa_attention_mixer(rms_norm(x, attn_norm[j]), attn_w_qkv[j],
                                        attn_q_norm[j], attn_k_norm[j], attn_w_o[j])
        else:
            x = x + conformer_conv_mixer(rms_norm(x, conv_norm[j]), conv_w_pw1[j], conv_b_pw1[j],
                                         conv_w_dw[j], conv_b_dw[j], conv_ln_g[j], conv_ln_b[j],
                                         conv_w_pw2[j], conv_b_pw2[j])
        x = x + conv_gated_mlp(rms_norm(x, ffn_norm[i]), ffn_w_up[i], ffn_w_dw[i],
                               ffn_b_dw[i], ffn_w_down[i])
    return x


def setup_inputs(seed: int = 0) -> dict:
    key = jax.random.key(seed)
    ks = jax.random.split(key, 24)
    f32 = jnp.float32

    def w(k, shape, fan_in):
        return jax.random.normal(k, shape, f32) * (fan_in ** -0.5)

    def gain(k, shape):
        return 1.0 + 0.01 * jax.random.normal(k, shape, f32)

    def bias(k, shape):
        return 0.02 * jax.random.normal(k, shape, f32)

    na, nc, d, f = N_ATTN_LAYERS, N_CONV_LAYERS, D_MODEL, D_FF
    return {
        "x_prompt": jax.random.normal(ks[0], (BATCH, SEQ, D_MODEL), f32),
        "x_sample": jax.random.normal(ks[1], (DEC_BATCH, DEC_SEQ, D_MODEL), f32),
        "attn_norm": gain(ks[2], (na, d)),
        "attn_w_qkv": w(ks[3], (na, d, QKV_DIM), d),
        "attn_q_norm": gain(ks[4], (na, HEAD_DIM)),
        "attn_k_norm": gain(ks[5], (na, HEAD_DIM)),
        "attn_w_o": w(ks[6], (na, ATTN_DIM, d), ATTN_DIM),
        "conv_norm": gain(ks[7], (nc, d)),
        "conv_w_pw1": w(ks[8], (nc, d, 2 * d), d),
        "conv_b_pw1": bias(ks[9], (nc, 2 * d)),
        "conv_w_dw": w(ks[10], (nc, CONV_K, d), CONV_K),
        "conv_b_dw": bias(ks[11], (nc, d)),
        "conv_ln_g": gain(ks[12], (nc, d)),
        "conv_ln_b": bias(ks[13], (nc, d)),
        "conv_w_pw2": w(ks[14], (nc, d, d), d),
        "conv_b_pw2": bias(ks[15], (nc, d)),
        "ffn_norm": gain(ks[16], (DEPTH, d)),
        "ffn_w_up": w(ks[17], (DEPTH, d, 2 * f), d),
        "ffn_w_dw": w(ks[18], (DEPTH, FFN_CONV_K, 2 * f), FFN_CONV_K),
        "ffn_b_dw": bias(ks[19], (DEPTH, 2 * f)),
        "ffn_w_down": w(ks[20], (DEPTH, f, d), f),
    }


def reference(x_prompt, x_sample, attn_norm, attn_w_qkv, attn_q_norm, attn_k_norm, attn_w_o,
              conv_norm, conv_w_pw1, conv_b_pw1, conv_w_dw, conv_b_dw,
              conv_ln_g, conv_ln_b, conv_w_pw2, conv_b_pw2,
              ffn_norm, ffn_w_up, ffn_w_dw, ffn_b_dw, ffn_w_down):
    y_prompt = run_trunk(x_prompt, attn_norm, attn_w_qkv, attn_q_norm, attn_k_norm, attn_w_o,
                         conv_norm, conv_w_pw1, conv_b_pw1, conv_w_dw, conv_b_dw,
                         conv_ln_g, conv_ln_b, conv_w_pw2, conv_b_pw2,
                         ffn_norm, ffn_w_up, ffn_w_dw, ffn_b_dw, ffn_w_down)
    y_sample = run_trunk(x_sample, attn_norm, attn_w_qkv, attn_q_norm, attn_k_norm, attn_w_o,
                         conv_norm, conv_w_pw1, conv_b_pw1, conv_w_dw, conv_b_dw,
                         conv_ln_g, conv_ln_b, conv_w_pw2, conv_b_pw2,
                         ffn_norm, ffn_w_up, ffn_w_dw, ffn_b_dw, ffn_w_down)
    return (y_prompt, y_sample)
```

```python
import functools

import jax
import jax.numpy as jnp
from jax import lax
from jax.experimental import pallas as pl
from jax.experimental.pallas import tpu as pltpu

N_HEADS = 16
N_KV_HEADS = 4
HEAD_DIM = 128
GQA_GROUP = N_HEADS // N_KV_HEADS
ATTN_DIM = N_HEADS * HEAD_DIM
KV_DIM = N_KV_HEADS * HEAD_DIM
ROPE_THETA = 10000.0
AXIS_DIM = HEAD_DIM // 2
AXIS_FREQS = AXIS_DIM // 2
GRID_W = 64
CONV_K = 31
FFN_CONV_K = 3
EPS = 1e-6

V7X_SUBLANES = 8
V7X_LANES = 128
V7X_VMEM_LIMIT_BYTES = 56 * 1024 * 1024

FFN_HALO = V7X_SUBLANES
CONV_HALO = 2 * V7X_SUBLANES

ATTN_TQ = 128
ATTN_TK = 2048

BF16 = jnp.bfloat16
F32 = jnp.float32


def _pick(total, preferred):
    if total <= preferred:
        return total
    t = preferred
    while t >= V7X_LANES:
        if total % t == 0:
            return t
        t -= V7X_LANES
    return total


def _params(n_grid):
    return pltpu.CompilerParams(
        dimension_semantics=("arbitrary",) * n_grid,
        vmem_limit_bytes=V7X_VMEM_LIMIT_BYTES)


def _resident(shape):
    return pl.BlockSpec(shape, lambda *_: (0,) * len(shape), pipeline_mode=pl.Buffered(1))


def _rms_norm(x, gain):
    y = x * lax.rsqrt(jnp.mean(x * x, axis=-1, keepdims=True) + EPS)
    return y * gain


def _sigmoid(x):
    return 1.0 / (1.0 + jnp.exp(-x))


def _dot(a, b):
    return jnp.dot(a, b, preferred_element_type=F32)


def _ffn_kernel(xp_ref, x_ref, xn_ref, gain_ref, wg_ref, wv_ref, dwg_ref, dwv_ref,
                bg_ref, bv_ref, wd_ref, o_ref, hn_ref, *, tm, n_seq_tiles):
    i = pl.program_id(1)
    j = pl.program_id(2)
    rows = tm + 2 * FFN_HALO

    @pl.when(j == 0)
    def _():
        gain = gain_ref[...]
        x = x_ref[0]
        hn_ref[FFN_HALO:FFN_HALO + tm, :] = _rms_norm(x, gain).astype(BF16)
        prev = jnp.where(i > 0, _rms_norm(xp_ref[0], gain), 0.0)
        hn_ref[0:FFN_HALO, :] = prev.astype(BF16)
        nxt = jnp.where(i < n_seq_tiles - 1, _rms_norm(xn_ref[0], gain), 0.0)
        hn_ref[FFN_HALO + tm:rows, :] = nxt.astype(BF16)
        o_ref[0] = x

    hn = hn_ref[...]

    def conv(w_ref, dw_ref, b_ref):
        u = _dot(hn, w_ref[...])
        dw = dw_ref[...]
        lo = pltpu.roll(u, 1, 0)
        hi = pltpu.roll(u, rows - 1, 0)
        s = slice(FFN_HALO, FFN_HALO + tm)
        return lo[s] * dw[0:1] + u[s] * dw[1:2] + hi[s] * dw[2:3] + b_ref[...]

    g = conv(wg_ref, dwg_ref, bg_ref)
    v = conv(wv_ref, dwv_ref, bv_ref)
    act = (g * _sigmoid(g)) * v
    o_ref[0] += _dot(act.astype(BF16), wd_ref[...])


def _ffn(x, gain, w_up, w_dw, b_dw, w_down):
    b, s, d = x.shape
    f = w_down.shape[0]
    tm = _pick(s, 512)
    tf = _pick(f, 512)
    n_seq, n_f = s // tm, f // tf
    hb = tm // FFN_HALO
    last_hb = s // FFN_HALO - 1
    gain = gain.reshape(1, d)
    b_dw = b_dw.reshape(1, 2 * f)

    return pl.pallas_call(
        functools.partial(_ffn_kernel, tm=tm, n_seq_tiles=n_seq),
        name="ffn",
        grid=(b, n_seq, n_f),
        in_specs=[
            pl.BlockSpec((1, FFN_HALO, d), lambda bi, i, j: (bi, jnp.maximum(i * hb - 1, 0), 0)),
            pl.BlockSpec((1, tm, d), lambda bi, i, j: (bi, i, 0)),
            pl.BlockSpec((1, FFN_HALO, d), lambda bi, i, j: (bi, jnp.minimum((i + 1) * hb, last_hb), 0)),
            pl.BlockSpec((1, d), lambda bi, i, j: (0, 0)),
            pl.BlockSpec((d, tf), lambda bi, i, j: (0, j)),
            pl.BlockSpec((d, tf), lambda bi, i, j: (0, n_f + j)),
            pl.BlockSpec((FFN_CONV_K, tf), lambda bi, i, j: (0, j)),
            pl.BlockSpec((FFN_CONV_K, tf), lambda bi, i, j: (0, n_f + j)),
            pl.BlockSpec((1, tf), lambda bi, i, j: (0, j)),
            pl.BlockSpec((1, tf), lambda bi, i, j: (0, n_f + j)),
            pl.BlockSpec((tf, d), lambda bi, i, j: (j, 0)),
        ],
        out_specs=pl.BlockSpec((1, tm, d), lambda bi, i, j: (bi, i, 0)),
        out_shape=jax.ShapeDtypeStruct((b, s, d), F32),
        scratch_shapes=[pltpu.VMEM((tm + 2 * FFN_HALO, d), BF16)],
        compiler_params=_params(3),
    )(x, x, x, gain, w_up, w_up, w_dw, w_dw, b_dw, b_dw, w_down)


def _rope_tables(seq_len):
    rows = seq_len // GRID_W
    row_pos = jnp.repeat(jnp.arange(rows, dtype=F32), GRID_W)
    col_pos = jnp.tile(jnp.arange(GRID_W, dtype=F32), rows)
    freqs = ROPE_THETA ** (-jnp.arange(AXIS_FREQS, dtype=F32) * 2.0 / AXIS_DIM)
    ang_r = row_pos[:, None] * freqs[None, :]
    ang_c = col_pos[:, None] * freqs[None, :]
    cr, sr, cc, sc = jnp.cos(ang_r), jnp.sin(ang_r), jnp.cos(ang_c), jnp.sin(ang_c)
    cos = jnp.concatenate([cr, cr, cc, cc], axis=-1)
    sin = jnp.concatenate([-sr, sr, -sc, sc], axis=-1)
    return cos, sin


def _qkv_kernel(x_ref, gain_ref, w_ref, qg_ref, kg_ref, cos_ref, sin_ref,
                q_ref, kt_ref, v_ref, *, scale):
    hn = _rms_norm(x_ref[0], gain_ref[...]).astype(BF16)
    qkv = _dot(hn, w_ref[...])
    cos = cos_ref[...]
    sin = sin_ref[...]
    lane = lax.broadcasted_iota(jnp.int32, cos.shape, 1)
    first = (lane % (2 * AXIS_FREQS)) < AXIS_FREQS

    def head(col, gain):
        y = _rms_norm(qkv[:, col:col + HEAD_DIM], gain)
        partner = jnp.where(first,
                            pltpu.roll(y, HEAD_DIM - AXIS_FREQS, 1),
                            pltpu.roll(y, AXIS_FREQS, 1))
        return y * cos + partner * sin

    qg = qg_ref[...]
    kg = kg_ref[...]
    for h in range(N_HEADS):
        c = h * HEAD_DIM
        q_ref[0, :, c:c + HEAD_DIM] = (head(c, qg) * scale).astype(BF16)
    for h in range(N_KV_HEADS):
        c = h * HEAD_DIM
        kt_ref[0, c:c + HEAD_DIM, :] = head(ATTN_DIM + c, kg).T.astype(BF16)
    v_ref[0] = qkv[:, ATTN_DIM + KV_DIM:].astype(BF16)


def _qkv(x, gain, w_qkv, q_gain, k_gain, cos, sin):
    b, s, d = x.shape
    tm = _pick(s, 512)
    n = w_qkv.shape[1]
    return pl.pallas_call(
        functools.partial(_qkv_kernel, scale=HEAD_DIM ** -0.5),
        name="qkv",
        grid=(b, s // tm),
        in_specs=[
            pl.BlockSpec((1, tm, d), lambda bi, i: (bi, i, 0)),
            _resident((1, d)),
            _resident((d, n)),
            _resident((1, HEAD_DIM)),
            _resident((1, HEAD_DIM)),
            pl.BlockSpec((tm, HEAD_DIM), lambda bi, i: (i, 0)),
            pl.BlockSpec((tm, HEAD_DIM), lambda bi, i: (i, 0)),
        ],
        out_specs=[
            pl.BlockSpec((1, tm, ATTN_DIM), lambda bi, i: (bi, i, 0)),
            pl.BlockSpec((1, KV_DIM, tm), lambda bi, i: (bi, 0, i)),
            pl.BlockSpec((1, tm, KV_DIM), lambda bi, i: (bi, i, 0)),
        ],
        out_shape=[
            jax.ShapeDtypeStruct((b, s, ATTN_DIM), BF16),
            jax.ShapeDtypeStruct((b, KV_DIM, s), BF16),
            jax.ShapeDtypeStruct((b, s, KV_DIM), BF16),
        ],
        compiler_params=_params(2),
    )(x, gain.reshape(1, d), w_qkv, q_gain.reshape(1, HEAD_DIM), k_gain.reshape(1, HEAD_DIM), cos, sin)


def _attn_kernel(q_ref, kt_ref, v_ref, o_ref, *, tq, tk, n_chunks):
    q = jnp.concatenate(
        [q_ref[0, :, g * HEAD_DIM:(g + 1) * HEAD_DIM] for g in range(GQA_GROUP)], axis=0)
    m = l = acc = None
    for c in range(n_chunks):
        s = _dot(q, kt_ref[0, :, c * tk:(c + 1) * tk])
        m_c = jnp.max(s, axis=-1, keepdims=True)
        if c == 0:
            m_new = m_c
            p = jnp.exp(s - m_new)
            l = jnp.sum(p, axis=-1, keepdims=True)
            acc = _dot(p.astype(BF16), v_ref[0, c * tk:(c + 1) * tk, :])
        else:
            m_new = jnp.maximum(m, m_c)
            alpha = jnp.exp(m - m_new)
            p = jnp.exp(s - m_new)
            l = alpha * l + jnp.sum(p, axis=-1, keepdims=True)
            acc = alpha * acc + _dot(p.astype(BF16), v_ref[0, c * tk:(c + 1) * tk, :])
        m = m_new
    out = acc / l
    for g in range(GQA_GROUP):
        o_ref[0, :, g * HEAD_DIM:(g + 1) * HEAD_DIM] = out[g * tq:(g + 1) * tq].astype(BF16)


def _attention(q, kt, v):
    b, s, _ = q.shape
    tq = _pick(s, ATTN_TQ)
    tk = _pick(s, ATTN_TK)
    gw = GQA_GROUP * HEAD_DIM
    return pl.pallas_call(
        functools.partial(_attn_kernel, tq=tq, tk=tk, n_chunks=s // tk),
        name="attn",
        grid=(b, N_KV_HEADS, s // tq),
        in_specs=[
            pl.BlockSpec((1, tq, gw), lambda bi, h, i: (bi, i, h)),
            pl.BlockSpec((1, HEAD_DIM, s), lambda bi, h, i: (bi, h, 0)),
            pl.BlockSpec((1, s, HEAD_DIM), lambda bi, h, i: (bi, 0, h)),
        ],
        out_specs=pl.BlockSpec((1, tq, gw), lambda bi, h, i: (bi, i, h)),
        out_shape=jax.ShapeDtypeStruct((b, s, ATTN_DIM), BF16),
        compiler_params=_params(3),
    )(q, kt, v)


def _oproj_kernel(a_ref, x_ref, w_ref, o_ref):
    o_ref[0] = x_ref[0] + _dot(a_ref[0], w_ref[...])


def _oproj(a, x, w_o):
    b, s, d = x.shape
    tm = _pick(s, 512)
    return pl.pallas_call(
        _oproj_kernel,
        name="oproj",
        grid=(b, s // tm),
        in_specs=[
            pl.BlockSpec((1, tm, ATTN_DIM), lambda bi, i: (bi, i, 0)),
            pl.BlockSpec((1, tm, d), lambda bi, i: (bi, i, 0)),
            _resident((ATTN_DIM, d)),
        ],
        out_specs=pl.BlockSpec((1, tm, d), lambda bi, i: (bi, i, 0)),
        out_shape=jax.ShapeDtypeStruct((b, s, d), F32),
        compiler_params=_params(2),
    )(a, x, w_o)


def _pw1_kernel(x_ref, gain_ref, w_ref, b_ref, u_ref, *, d):
    hn = _rms_norm(x_ref[0], gain_ref[...]).astype(BF16)
    u = _dot(hn, w_ref[...]) + b_ref[...]
    u_ref[0] = u[:, :d] * _sigmoid(u[:, d:])


def _pw1(x, gain, w_pw1, b_pw1):
    b, s, d = x.shape
    tm = _pick(s, 512)
    return pl.pallas_call(
        functools.partial(_pw1_kernel, d=d),
        name="pw1",
        grid=(b, s // tm),
        in_specs=[
            pl.BlockSpec((1, tm, d), lambda bi, i: (bi, i, 0)),
            _resident((1, d)),
            _resident((d, 2 * d)),
            _resident((1, 2 * d)),
        ],
        out_specs=pl.BlockSpec((1, tm, d), lambda bi, i: (bi, i, 0)),
        out_shape=jax.ShapeDtypeStruct((b, s, d), F32),
        compiler_params=_params(2),
    )(x, gain.reshape(1, d), w_pw1, b_pw1.reshape(1, 2 * d))


def _convpw2_kernel(up_ref, u_ref, un_ref, x_ref, wdw_ref, bdw_ref, lng_ref, lnb_ref,
                    w2_ref, b2_ref, o_ref, ext_ref, conv_ref, *, tm, cw, n_seq_tiles):
    i = pl.program_id(1)
    rows = tm + 2 * CONV_HALO
    d = conv_ref.shape[1]
    ext_ref[0:CONV_HALO, :] = jnp.where(i > 0, up_ref[0], 0.0)
    ext_ref[CONV_HALO:CONV_HALO + tm, :] = u_ref[0]
    ext_ref[CONV_HALO + tm:rows, :] = jnp.where(i < n_seq_tiles - 1, un_ref[0], 0.0)

    base = CONV_HALO - CONV_K // 2
    for c in range(d // cw):
        cols = slice(c * cw, (c + 1) * cw)
        ue = ext_ref[:, cols]
        acc = None
        for r in range(V7X_SUBLANES):
            ur = ue if r == 0 else pltpu.roll(ue, rows - r, 0)
            for a in range(2 * CONV_HALO // V7X_SUBLANES):
                k = a * V7X_SUBLANES + r - base
                if 0 <= k < CONV_K:
                    term = ur[a * V7X_SUBLANES:a * V7X_SUBLANES + tm] * wdw_ref[k:k + 1, cols]
                    acc = term if acc is None else acc + term
        conv_ref[:, cols] = acc + bdw_ref[:, cols]

    y = conv_ref[...]
    mu = jnp.mean(y, axis=-1, keepdims=True)
    yc = y - mu
    var = jnp.mean(yc * yc, axis=-1, keepdims=True)
    y = yc * lax.rsqrt(var + EPS) * lng_ref[...] + lnb_ref[...]
    y = y * _sigmoid(y)
    o_ref[0] = x_ref[0] + (_dot(y.astype(BF16), w2_ref[...]) + b2_ref[...])


def _convpw2(u, x, w_dw, b_dw, ln_g, ln_b, w_pw2, b_pw2):
    b, s, d = x.shape
    tm = _pick(s, 512)
    cw = _pick(d, 512)
    n_seq = s // tm
    hb = tm // CONV_HALO
    last_hb = s // CONV_HALO - 1
    row = lambda a: a.reshape(1, d)
    return pl.pallas_call(
        functools.partial(_convpw2_kernel, tm=tm, cw=cw, n_seq_tiles=n_seq),
        name="convpw2",
        grid=(b, n_seq),
        in_specs=[
            pl.BlockSpec((1, CONV_HALO, d), lambda bi, i: (bi, jnp.maximum(i * hb - 1, 0), 0)),
            pl.BlockSpec((1, tm, d), lambda bi, i: (bi, i, 0)),
            pl.BlockSpec((1, CONV_HALO, d), lambda bi, i: (bi, jnp.minimum((i + 1) * hb, last_hb), 0)),
            pl.BlockSpec((1, tm, d), lambda bi, i: (bi, i, 0)),
            _resident((CONV_K, d)),
            _resident((1, d)),
            _resident((1, d)),
            _resident((1, d)),
            _resident((d, d)),
            _resident((1, d)),
        ],
        out_specs=pl.BlockSpec((1, tm, d), lambda bi, i: (bi, i, 0)),
        out_shape=jax.ShapeDtypeStruct((b, s, d), F32),
        scratch_shapes=[pltpu.VMEM((tm + 2 * CONV_HALO, d), F32), pltpu.VMEM((tm, d), F32)],
        compiler_params=_params(2),
    )(u, u, u, x, w_dw, row(b_dw), row(ln_g), row(ln_b), w_pw2, row(b_pw2))


def _trunk(x, p):
    depth = p["ffn_norm"].shape[0]
    cos, sin = _rope_tables(x.shape[1])
    for i in range(depth):
        j = i // 2
        if i % 2 == 0:
            q, kt, v = _qkv(x, p["attn_norm"][j], p["attn_w_qkv"][j], p["attn_q_norm"][j],
                            p["attn_k_norm"][j], cos, sin)
            x = _oproj(_attention(q, kt, v), x, p["attn_w_o"][j])
        else:
            u = _pw1(x, p["conv_norm"][j], p["conv_w_pw1"][j], p["conv_b_pw1"][j])
            x = _convpw2(u, x, p["conv_w_dw"][j], p["conv_b_dw"][j], p["conv_ln_g"][j],
                         p["conv_ln_b"][j], p["conv_w_pw2"][j], p["conv_b_pw2"][j])
        x = _ffn(x, p["ffn_norm"][i], p["ffn_w_up"][i], p["ffn_w_dw"][i], p["ffn_b_dw"][i],
                 p["ffn_w_down"][i])
    return x


def kernel(x_prompt, x_sample, attn_norm, attn_w_qkv, attn_q_norm, attn_k_norm, attn_w_o,
           conv_norm, conv_w_pw1, conv_b_pw1, conv_w_dw, conv_b_dw, conv_ln_g, conv_ln_b,
           conv_w_pw2, conv_b_pw2, ffn_norm, ffn_w_up, ffn_w_dw, ffn_b_dw, ffn_w_down):
    p = dict(
        attn_norm=attn_norm, attn_w_qkv=attn_w_qkv.astype(BF16), attn_q_norm=attn_q_norm,
        attn_k_norm=attn_k_norm, attn_w_o=attn_w_o.astype(BF16),
        conv_norm=conv_norm, conv_w_pw1=conv_w_pw1.astype(BF16), conv_b_pw1=conv_b_pw1,
        conv_w_dw=conv_w_dw, conv_b_dw=conv_b_dw, conv_ln_g=conv_ln_g, conv_ln_b=conv_ln_b,
        conv_w_pw2=conv_w_pw2.astype(BF16), conv_b_pw2=conv_b_pw2,
        ffn_norm=ffn_norm, ffn_w_up=ffn_w_up.astype(BF16), ffn_w_dw=ffn_w_dw, ffn_b_dw=ffn_b_dw,
        ffn_w_down=ffn_w_down.astype(BF16))
    return _trunk(x_prompt, p), _trunk(x_sample, p)
```

```python
import functools

import jax
import jax.numpy as jnp
from jax import lax
from jax.experimental import pallas as pl
from jax.experimental.pallas import tpu as pltpu

N_HEADS = 16
N_KV_HEADS = 4
HEAD_DIM = 128
GQA_GROUP = N_HEADS // N_KV_HEADS
ATTN_DIM = N_HEADS * HEAD_DIM
KV_DIM = N_KV_HEADS * HEAD_DIM
ROPE_THETA = 10000.0
AXIS_DIM = HEAD_DIM // 2
AXIS_FREQS = AXIS_DIM // 2
GRID_W = 64
CONV_K = 31
FFN_CONV_K = 3
EPS = 1e-6

V7X_SUBLANES = 8
V7X_LANES = 128
V7X_VMEM_LIMIT_BYTES = 56 * 1024 * 1024

FFN_HALO = V7X_SUBLANES
CONV_HALO = 2 * V7X_SUBLANES

ATTN_TQ = 128
ATTN_TK = 1024

BF16 = jnp.bfloat16
F32 = jnp.float32


def _pick(total, preferred):
    if total <= preferred:
        return total
    t = preferred
    while t >= V7X_LANES:
        if total % t == 0:
            return t
        t -= V7X_LANES
    return total


def _params(n_grid):
    return pltpu.CompilerParams(
        dimension_semantics=("arbitrary",) * n_grid,
        vmem_limit_bytes=V7X_VMEM_LIMIT_BYTES)


def _resident(shape):
    return pl.BlockSpec(shape, lambda *_: (0,) * len(shape), pipeline_mode=pl.Buffered(1))


def _rms_norm(x, gain):
    y = x * lax.rsqrt(jnp.mean(x * x, axis=-1, keepdims=True) + EPS)
    return y * gain


def _sigmoid(x):
    return 1.0 / (1.0 + jnp.exp(-x))


def _dot(a, b):
    return jnp.dot(a, b, preferred_element_type=F32)


def _ffn_kernel(xp_ref, x_ref, xn_ref, gain_ref, wg_ref, wv_ref, dwg_ref, dwv_ref,
                bg_ref, bv_ref, wd_ref, o_ref, hn_ref, act_ref, *, tm, n_seq_tiles, n_f):
    i = pl.program_id(1)
    j = pl.program_id(2)
    rows = tm + 2 * FFN_HALO

    def conv(w_ref, dw_ref, b_ref):
        u = _dot(hn_ref[...], w_ref[...])
        dw = dw_ref[...]
        lo = pltpu.roll(u, 1, 0)
        hi = pltpu.roll(u, rows - 1, 0)
        s = slice(FFN_HALO, FFN_HALO + tm)
        return lo[s] * dw[0:1] + u[s] * dw[1:2] + hi[s] * dw[2:3] + b_ref[...]

    def up():
        g = conv(wg_ref, dwg_ref, bg_ref)
        v = conv(wv_ref, dwv_ref, bv_ref)
        return ((g * _sigmoid(g)) * v).astype(BF16)

    def down(slot):
        o_ref[0] += _dot(act_ref[slot], wd_ref[...])

    @pl.when(j == 0)
    def _():
        gain = gain_ref[...]
        x = x_ref[0]
        hn_ref[FFN_HALO:FFN_HALO + tm, :] = _rms_norm(x, gain).astype(BF16)
        prev = jnp.where(i > 0, _rms_norm(xp_ref[0], gain), 0.0)
        hn_ref[0:FFN_HALO, :] = prev.astype(BF16)
        nxt = jnp.where(i < n_seq_tiles - 1, _rms_norm(xn_ref[0], gain), 0.0)
        hn_ref[FFN_HALO + tm:rows, :] = nxt.astype(BF16)
        o_ref[0] = x
        act_ref[0] = up()

    for parity in (0, 1):
        @pl.when((j > 0) & (j < n_f) & (j % 2 == parity))
        def _():
            act = up()
            down(1 - parity)
            act_ref[parity] = act

    @pl.when(j == n_f)
    def _():
        down((n_f - 1) % 2)


def _ffn(x, gain, w_up, w_dw, b_dw, w_down):
    b, s, d = x.shape
    f = w_down.shape[0]
    tm = _pick(s, 512)
    tf = _pick(f, 512)
    n_seq, n_f = s // tm, f // tf
    hb = tm // FFN_HALO
    last_hb = s // FFN_HALO - 1
    gain = gain.reshape(1, d)
    b_dw = b_dw.reshape(1, 2 * f)
    up_j = lambda j: jnp.minimum(j, n_f - 1)
    down_j = lambda j: jnp.maximum(j - 1, 0)

    return pl.pallas_call(
        functools.partial(_ffn_kernel, tm=tm, n_seq_tiles=n_seq, n_f=n_f),
        name="ffn",
        grid=(b, n_seq, n_f + 1),
        in_specs=[
            pl.BlockSpec((1, FFN_HALO, d), lambda bi, i, j: (bi, jnp.maximum(i * hb - 1, 0), 0)),
            pl.BlockSpec((1, tm, d), lambda bi, i, j: (bi, i, 0)),
            pl.BlockSpec((1, FFN_HALO, d), lambda bi, i, j: (bi, jnp.minimum((i + 1) * hb, last_hb), 0)),
            pl.BlockSpec((1, d), lambda bi, i, j: (0, 0)),
            pl.BlockSpec((d, tf), lambda bi, i, j: (0, up_j(j))),
            pl.BlockSpec((d, tf), lambda bi, i, j: (0, n_f + up_j(j))),
            pl.BlockSpec((FFN_CONV_K, tf), lambda bi, i, j: (0, up_j(j))),
            pl.BlockSpec((FFN_CONV_K, tf), lambda bi, i, j: (0, n_f + up_j(j))),
            pl.BlockSpec((1, tf), lambda bi, i, j: (0, up_j(j))),
            pl.BlockSpec((1, tf), lambda bi, i, j: (0, n_f + up_j(j))),
            pl.BlockSpec((tf, d), lambda bi, i, j: (down_j(j), 0)),
        ],
        out_specs=pl.BlockSpec((1, tm, d), lambda bi, i, j: (bi, i, 0)),
        out_shape=jax.ShapeDtypeStruct((b, s, d), F32),
        scratch_shapes=[pltpu.VMEM((tm + 2 * FFN_HALO, d), BF16), pltpu.VMEM((2, tm, tf), BF16)],
        compiler_params=_params(3),
    )(x, x, x, gain, w_up, w_up, w_dw, w_dw, b_dw, b_dw, w_down)


def _rope_tables(seq_len):
    rows = seq_len // GRID_W
    row_pos = jnp.repeat(jnp.arange(rows, dtype=F32), GRID_W)
    col_pos = jnp.tile(jnp.arange(GRID_W, dtype=F32), rows)
    freqs = ROPE_THETA ** (-jnp.arange(AXIS_FREQS, dtype=F32) * 2.0 / AXIS_DIM)
    ang_r = row_pos[:, None] * freqs[None, :]
    ang_c = col_pos[:, None] * freqs[None, :]
    cr, sr, cc, sc = jnp.cos(ang_r), jnp.sin(ang_r), jnp.cos(ang_c), jnp.sin(ang_c)
    cos = jnp.concatenate([cr, cr, cc, cc], axis=-1)
    sin = jnp.concatenate([-sr, sr, -sc, sc], axis=-1)
    return cos, sin


def _qkv_kernel(x_ref, gain_ref, w_ref, qg_ref, kg_ref, cos_ref, sin_ref,
                q_ref, kt_ref, v_ref, *, scale):
    hn = _rms_norm(x_ref[0], gain_ref[...]).astype(BF16)
    qkv = _dot(hn, w_ref[...])
    cos = cos_ref[...]
    sin = sin_ref[...]
    lane = lax.broadcasted_iota(jnp.int32, cos.shape, 1)
    first = (lane % (2 * AXIS_FREQS)) < AXIS_FREQS

    def head(col, gain):
        y = _rms_norm(qkv[:, col:col + HEAD_DIM], gain)
        partner = jnp.where(first,
                            pltpu.roll(y, HEAD_DIM - AXIS_FREQS, 1),
                            pltpu.roll(y, AXIS_FREQS, 1))
        return y * cos + partner * sin

    qg = qg_ref[...]
    kg = kg_ref[...]
    for h in range(N_HEADS):
        c = h * HEAD_DIM
        q_ref[0, :, c:c + HEAD_DIM] = (head(c, qg) * scale).astype(BF16)
    for h in range(N_KV_HEADS):
        c = h * HEAD_DIM
        kt_ref[0, c:c + HEAD_DIM, :] = head(ATTN_DIM + c, kg).T.astype(BF16)
    v_ref[0] = qkv[:, ATTN_DIM + KV_DIM:].astype(BF16)


def _qkv(x, gain, w_qkv, q_gain, k_gain, cos, sin):
    b, s, d = x.shape
    tm = _pick(s, 512)
    n = w_qkv.shape[1]
    return pl.pallas_call(
        functools.partial(_qkv_kernel, scale=HEAD_DIM ** -0.5),
        name="qkv",
        grid=(b, s // tm),
        in_specs=[
            pl.BlockSpec((1, tm, d), lambda bi, i: (bi, i, 0)),
            _resident((1, d)),
            _resident((d, n)),
            _resident((1, HEAD_DIM)),
            _resident((1, HEAD_DIM)),
            pl.BlockSpec((tm, HEAD_DIM), lambda bi, i: (i, 0)),
            pl.BlockSpec((tm, HEAD_DIM), lambda bi, i: (i, 0)),
        ],
        out_specs=[
            pl.BlockSpec((1, tm, ATTN_DIM), lambda bi, i: (bi, i, 0)),
            pl.BlockSpec((1, KV_DIM, tm), lambda bi, i: (bi, 0, i)),
            pl.BlockSpec((1, tm, KV_DIM), lambda bi, i: (bi, i, 0)),
        ],
        out_shape=[
            jax.ShapeDtypeStruct((b, s, ATTN_DIM), BF16),
            jax.ShapeDtypeStruct((b, KV_DIM, s), BF16),
            jax.ShapeDtypeStruct((b, s, KV_DIM), BF16),
        ],
        compiler_params=_params(2),
    )(x, gain.reshape(1, d), w_qkv, q_gain.reshape(1, HEAD_DIM), k_gain.reshape(1, HEAD_DIM), cos, sin)


def _attn_kernel(q_ref, kt_ref, v_ref, o_ref, *, tq, tk, n_chunks):
    q = jnp.concatenate(
        [q_ref[0, :, g * HEAD_DIM:(g + 1) * HEAD_DIM] for g in range(GQA_GROUP)], axis=0)
    m = l = acc = None
    for c in range(n_chunks):
        s = _dot(q, kt_ref[0, :, c * tk:(c + 1) * tk])
        m_c = jnp.max(s, axis=-1, keepdims=True)
        if c == 0:
            m_new = m_c
            p = jnp.exp(s - m_new)
            l = jnp.sum(p, axis=-1, keepdims=True)
            acc = _dot(p.astype(BF16), v_ref[0, c * tk:(c + 1) * tk, :])
        else:
            m_new = jnp.maximum(m, m_c)
            alpha = jnp.exp(m - m_new)
            p = jnp.exp(s - m_new)
            l = alpha * l + jnp.sum(p, axis=-1, keepdims=True)
            acc = alpha * acc + _dot(p.astype(BF16), v_ref[0, c * tk:(c + 1) * tk, :])
        m = m_new
    out = acc / l
    for g in range(GQA_GROUP):
        o_ref[0, :, g * HEAD_DIM:(g + 1) * HEAD_DIM] = out[g * tq:(g + 1) * tq].astype(BF16)


def _attention(q, kt, v):
    b, s, _ = q.shape
    tq = _pick(s, ATTN_TQ)
    tk = _pick(s, ATTN_TK)
    gw = GQA_GROUP * HEAD_DIM
    return pl.pallas_call(
        functools.partial(_attn_kernel, tq=tq, tk=tk, n_chunks=s // tk),
        name="attn",
        grid=(b, N_KV_HEADS, s // tq),
        in_specs=[
            pl.BlockSpec((1, tq, gw), lambda bi, h, i: (bi, i, h)),
            pl.BlockSpec((1, HEAD_DIM, s), lambda bi, h, i: (bi, h, 0)),
            pl.BlockSpec((1, s, HEAD_DIM), lambda bi, h, i: (bi, 0, h)),
        ],
        out_specs=pl.BlockSpec((1, tq, gw), lambda bi, h, i: (bi, i, h)),
        out_shape=jax.ShapeDtypeStruct((b, s, ATTN_DIM), BF16),
        compiler_params=_params(3),
    )(q, kt, v)


def _oproj_kernel(a_ref, x_ref, w_ref, o_ref):
    o_ref[0] = x_ref[0] + _dot(a_ref[0], w_ref[...])


def _oproj(a, x, w_o):
    b, s, d = x.shape
    tm = _pick(s, 512)
    return pl.pallas_call(
        _oproj_kernel,
        name="oproj",
        grid=(b, s // tm),
        in_specs=[
            pl.BlockSpec((1, tm, ATTN_DIM), lambda bi, i: (bi, i, 0)),
            pl.BlockSpec((1, tm, d), lambda bi, i: (bi, i, 0)),
            _resident((ATTN_DIM, d)),
        ],
        out_specs=pl.BlockSpec((1, tm, d), lambda bi, i: (bi, i, 0)),
        out_shape=jax.ShapeDtypeStruct((b, s, d), F32),
        compiler_params=_params(2),
    )(a, x, w_o)


def _pw1_kernel(x_ref, gain_ref, w_ref, b_ref, u_ref, *, d):
    hn = _rms_norm(x_ref[0], gain_ref[...]).astype(BF16)
    u = _dot(hn, w_ref[...]) + b_ref[...]
    u_ref[0] = u[:, :d] * _sigmoid(u[:, d:])


def _pw1(x, gain, w_pw1, b_pw1):
    b, s, d = x.shape
    tm = _pick(s, 512)
    return pl.pallas_call(
        functools.partial(_pw1_kernel, d=d),
        name="pw1",
        grid=(b, s // tm),
        in_specs=[
            pl.BlockSpec((1, tm, d), lambda bi, i: (bi, i, 0)),
            _resident((1, d)),
            _resident((d, 2 * d)),
            _resident((1, 2 * d)),
        ],
        out_specs=pl.BlockSpec((1, tm, d), lambda bi, i: (bi, i, 0)),
        out_shape=jax.ShapeDtypeStruct((b, s, d), F32),
        compiler_params=_params(2),
    )(x, gain.reshape(1, d), w_pw1, b_pw1.reshape(1, 2 * d))


def _convpw2_kernel(up_ref, u_ref, un_ref, x_ref, wdw_ref, bdw_ref, lng_ref, lnb_ref,
                    w2_ref, b2_ref, o_ref, ext_ref, conv_ref, *, tm, cw, n_seq_tiles):
    i = pl.program_id(1)
    rows = tm + 2 * CONV_HALO
    d = conv_ref.shape[1]
    ext_ref[0:CONV_HALO, :] = jnp.where(i > 0, up_ref[0], 0.0)
    ext_ref[CONV_HALO:CONV_HALO + tm, :] = u_ref[0]
    ext_ref[CONV_HALO + tm:rows, :] = jnp.where(i < n_seq_tiles - 1, un_ref[0], 0.0)

    base = CONV_HALO - CONV_K // 2
    for c in range(d // cw):
        cols = slice(c * cw, (c + 1) * cw)
        ue = ext_ref[:, cols]
        acc = None
        for r in range(V7X_SUBLANES):
            ur = ue if r == 0 else pltpu.roll(ue, rows - r, 0)
            for a in range(2 * CONV_HALO // V7X_SUBLANES):
                k = a * V7X_SUBLANES + r - base
                if 0 <= k < CONV_K:
                    term = ur[a * V7X_SUBLANES:a * V7X_SUBLANES + tm] * wdw_ref[k:k + 1, cols]
                    acc = term if acc is None else acc + term
        conv_ref[:, cols] = acc + bdw_ref[:, cols]

    y = conv_ref[...]
    mu = jnp.mean(y, axis=-1, keepdims=True)
    yc = y - mu
    var = jnp.mean(yc * yc, axis=-1, keepdims=True)
    y = yc * lax.rsqrt(var + EPS) * lng_ref[...] + lnb_ref[...]
    y = y * _sigmoid(y)
    o_ref[0] = x_ref[0] + (_dot(y.astype(BF16), w2_ref[...]) + b2_ref[...])


def _convpw2(u, x, w_dw, b_dw, ln_g, ln_b, w_pw2, b_pw2):
    b, s, d = x.shape
    tm = _pick(s, 512)
    cw = _pick(d, 512)
    n_seq = s // tm
    hb = tm // CONV_HALO
    last_hb = s // CONV_HALO - 1
    row = lambda a: a.reshape(1, d)
    return pl.pallas_call(
        functools.partial(_convpw2_kernel, tm=tm, cw=cw, n_seq_tiles=n_seq),
        name="convpw2",
        grid=(b, n_seq),
        in_specs=[
            pl.BlockSpec((1, CONV_HALO, d), lambda bi, i: (bi, jnp.maximum(i * hb - 1, 0), 0)),
            pl.BlockSpec((1, tm, d), lambda bi, i: (bi, i, 0)),
            pl.BlockSpec((1, CONV_HALO, d), lambda bi, i: (bi, jnp.minimum((i + 1) * hb, last_hb), 0)),
            pl.BlockSpec((1, tm, d), lambda bi, i: (bi, i, 0)),
            _resident((CONV_K, d)),
            _resident((1, d)),
            _resident((1, d)),
            _resident((1, d)),
            _resident((d, d)),
            _resident((1, d)),
        ],
        out_specs=pl.BlockSpec((1, tm, d), lambda bi, i: (bi, i, 0)),
        out_shape=jax.ShapeDtypeStruct((b, s, d), F32),
        scratch_shapes=[pltpu.VMEM((tm + 2 * CONV_HALO, d), F32), pltpu.VMEM((tm, d), F32)],
        compiler_params=_params(2),
    )(u, u, u, x, w_dw, row(b_dw), row(ln_g), row(ln_b), w_pw2, row(b_pw2))


def _trunk(x, p):
    depth = p["ffn_norm"].shape[0]
    cos, sin = _rope_tables(x.shape[1])
    for i in range(depth):
        j = i // 2
        if i % 2 == 0:
            q, kt, v = _qkv(x, p["attn_norm"][j], p["attn_w_qkv"][j], p["attn_q_norm"][j],
                            p["attn_k_norm"][j], cos, sin)
            x = _oproj(_attention(q, kt, v), x, p["attn_w_o"][j])
        else:
            u = _pw1(x, p["conv_norm"][j], p["conv_w_pw1"][j], p["conv_b_pw1"][j])
            x = _convpw2(u, x, p["conv_w_dw"][j], p["conv_b_dw"][j], p["conv_ln_g"][j],
                         p["conv_ln_b"][j], p["conv_w_pw2"][j], p["conv_b_pw2"][j])
        x = _ffn(x, p["ffn_norm"][i], p["ffn_w_up"][i], p["ffn_w_dw"][i], p["ffn_b_dw"][i],
                 p["ffn_w_down"][i])
    return x


def kernel(x_prompt, x_sample, attn_norm, attn_w_qkv, attn_q_norm, attn_k_norm, attn_w_o,
           conv_norm, conv_w_pw1, conv_b_pw1, conv_w_dw, conv_b_dw, conv_ln_g, conv_ln_b,
           conv_w_pw2, conv_b_pw2, ffn_norm, ffn_w_up, ffn_w_dw, ffn_b_dw, ffn_w_down):
    p = dict(
        attn_norm=attn_norm, attn_w_qkv=attn_w_qkv.astype(BF16), attn_q_norm=attn_q_norm,
        attn_k_norm=attn_k_norm, attn_w_o=attn_w_o.astype(BF16),
        conv_norm=conv_norm, conv_w_pw1=conv_w_pw1.astype(BF16), conv_b_pw1=conv_b_pw1,
        conv_w_dw=conv_w_dw, conv_b_dw=conv_b_dw, conv_ln_g=conv_ln_g, conv_ln_b=conv_ln_b,
        conv_w_pw2=conv_w_pw2.astype(BF16), conv_b_pw2=conv_b_pw2,
        ffn_norm=ffn_norm, ffn_w_up=ffn_w_up.astype(BF16), ffn_w_dw=ffn_w_dw, ffn_b_dw=ffn_b_dw,
        ffn_w_down=ffn_w_down.astype(BF16))
    return _trunk(x_prompt, p), _trunk(x_sample, p)
```

```python
import functools

import jax
import jax.numpy as jnp
from jax import lax
from jax.experimental import pallas as pl
from jax.experimental.pallas import tpu as pltpu

N_HEADS = 16
N_KV_HEADS = 4
HEAD_DIM = 128
GQA_GROUP = N_HEADS // N_KV_HEADS
ATTN_DIM = N_HEADS * HEAD_DIM
KV_DIM = N_KV_HEADS * HEAD_DIM
ROPE_THETA = 10000.0
AXIS_DIM = HEAD_DIM // 2
AXIS_FREQS = AXIS_DIM // 2
GRID_W = 64
CONV_K = 31
FFN_CONV_K = 3
EPS = 1e-6

V7X_SUBLANES = 8
V7X_LANES = 128
V7X_VMEM_LIMIT_BYTES = 56 * 1024 * 1024

FFN_HALO = V7X_SUBLANES
CONV_HALO = 2 * V7X_SUBLANES

ATTN_TQ = 256
ATTN_TK = 2048
ROW_SUB_TILES = 2
LOG2_E = 1.4426950408889634

BF16 = jnp.bfloat16
F32 = jnp.float32


def _pick(total, preferred):
    if total <= preferred:
        return total
    t = preferred
    while t >= V7X_LANES:
        if total % t == 0:
            return t
        t -= V7X_LANES
    return total


def _params(n_grid):
    return pltpu.CompilerParams(
        dimension_semantics=("arbitrary",) * n_grid,
        vmem_limit_bytes=V7X_VMEM_LIMIT_BYTES)


def _resident(shape):
    return pl.BlockSpec(shape, lambda *_: (0,) * len(shape), pipeline_mode=pl.Buffered(1))


def _layer_resident(shape, layer):
    return pl.BlockSpec((None,) + shape, lambda *_: (layer,) + (0,) * len(shape),
                        pipeline_mode=pl.Buffered(1))


def _rms_norm(x, gain):
    y = x * lax.rsqrt(jnp.mean(x * x, axis=-1, keepdims=True) + EPS)
    return y * gain


def _sigmoid(x):
    return 1.0 / (1.0 + jnp.exp(-x))


def _dot(a, b):
    return jnp.dot(a, b, preferred_element_type=F32)


def _ffn_kernel(xp_ref, x_ref, xn_ref, gain_ref, wg_ref, wv_ref, dwg_ref, dwv_ref,
                bg_ref, bv_ref, wd_ref, o_ref, hn_ref, *, tm, n_seq_tiles):
    i = pl.program_id(1)
    j = pl.program_id(2)
    rows = tm + 2 * FFN_HALO

    @pl.when(j == 0)
    def _():
        gain = gain_ref[...]
        x = x_ref[0]
        hn_ref[FFN_HALO:FFN_HALO + tm, :] = _rms_norm(x, gain).astype(BF16)
        prev = jnp.where(i > 0, _rms_norm(xp_ref[0], gain), 0.0)
        hn_ref[0:FFN_HALO, :] = prev.astype(BF16)
        nxt = jnp.where(i < n_seq_tiles - 1, _rms_norm(xn_ref[0], gain), 0.0)
        hn_ref[FFN_HALO + tm:rows, :] = nxt.astype(BF16)
        o_ref[0] = x

    hn = hn_ref[...]

    def conv(w_ref, dw_ref, b_ref):
        u = _dot(hn, w_ref[...])
        dw = dw_ref[...]
        lo = pltpu.roll(u, 1, 0)
        hi = pltpu.roll(u, rows - 1, 0)
        s = slice(FFN_HALO, FFN_HALO + tm)
        return lo[s] * dw[0:1] + u[s] * dw[1:2] + hi[s] * dw[2:3] + b_ref[...]

    g = conv(wg_ref, dwg_ref, bg_ref)
    v = conv(wv_ref, dwv_ref, bv_ref)
    act = (g * _sigmoid(g)) * v
    o_ref[0] += _dot(act.astype(BF16), wd_ref[...])


def _ffn(x, gain, w_up, w_dw, b_dw, w_down, layer):
    b, s, d = x.shape
    f = w_down.shape[1]
    tm = _pick(s, 512)
    tf = _pick(f, 512)
    n_seq, n_f = s // tm, f // tf
    hb = tm // FFN_HALO
    last_hb = s // FFN_HALO - 1
    gain = gain.reshape(1, d)
    b_dw = b_dw.reshape(1, 2 * f)

    return pl.pallas_call(
        functools.partial(_ffn_kernel, tm=tm, n_seq_tiles=n_seq),
        name="ffn",
        grid=(b, n_seq, n_f),
        in_specs=[
            pl.BlockSpec((1, FFN_HALO, d), lambda bi, i, j: (bi, jnp.maximum(i * hb - 1, 0), 0)),
            pl.BlockSpec((1, tm, d), lambda bi, i, j: (bi, i, 0)),
            pl.BlockSpec((1, FFN_HALO, d), lambda bi, i, j: (bi, jnp.minimum((i + 1) * hb, last_hb), 0)),
            pl.BlockSpec((1, d), lambda bi, i, j: (0, 0)),
            pl.BlockSpec((None, d, tf), lambda bi, i, j: (layer, 0, j)),
            pl.BlockSpec((None, d, tf), lambda bi, i, j: (layer, 0, n_f + j)),
            pl.BlockSpec((FFN_CONV_K, tf), lambda bi, i, j: (0, j)),
            pl.BlockSpec((FFN_CONV_K, tf), lambda bi, i, j: (0, n_f + j)),
            pl.BlockSpec((1, tf), lambda bi, i, j: (0, j)),
            pl.BlockSpec((1, tf), lambda bi, i, j: (0, n_f + j)),
            pl.BlockSpec((None, tf, d), lambda bi, i, j: (layer, j, 0)),
        ],
        out_specs=pl.BlockSpec((1, tm, d), lambda bi, i, j: (bi, i, 0)),
        out_shape=jax.ShapeDtypeStruct((b, s, d), F32),
        scratch_shapes=[pltpu.VMEM((tm + 2 * FFN_HALO, d), BF16)],
        compiler_params=_params(3),
    )(x, x, x, gain, w_up, w_up, w_dw, w_dw, b_dw, b_dw, w_down)


def _rope_tables(seq_len):
    rows = seq_len // GRID_W
    row_pos = jnp.repeat(jnp.arange(rows, dtype=F32), GRID_W)
    col_pos = jnp.tile(jnp.arange(GRID_W, dtype=F32), rows)
    freqs = ROPE_THETA ** (-jnp.arange(AXIS_FREQS, dtype=F32) * 2.0 / AXIS_DIM)
    ang_r = row_pos[:, None] * freqs[None, :]
    ang_c = col_pos[:, None] * freqs[None, :]
    cr, sr, cc, sc = jnp.cos(ang_r), jnp.sin(ang_r), jnp.cos(ang_c), jnp.sin(ang_c)
    cos = jnp.concatenate([cr, cr, cc, cc], axis=-1)
    sin = jnp.concatenate([-sr, sr, -sc, sc], axis=-1)
    return cos, sin


def _qkv_kernel(x_ref, gain_ref, w_ref, qg_ref, kg_ref, cos_ref, sin_ref,
                q_ref, kt_ref, v_ref, *, scale):
    ts = x_ref.shape[1] // ROW_SUB_TILES
    qg = qg_ref[...]
    kg = kg_ref[...]
    lane = lax.broadcasted_iota(jnp.int32, (ts, HEAD_DIM), 1)
    first = (lane % (2 * AXIS_FREQS)) < AXIS_FREQS
    for c in range(ROW_SUB_TILES):
        rows = slice(c * ts, (c + 1) * ts)
        hn = _rms_norm(x_ref[0, rows, :], gain_ref[...]).astype(BF16)
        qkv = _dot(hn, w_ref[...])
        cos = cos_ref[rows, :]
        sin = sin_ref[rows, :]

        def head(col, gain):
            y = _rms_norm(qkv[:, col:col + HEAD_DIM], gain)
            partner = jnp.where(first,
                                pltpu.roll(y, HEAD_DIM - AXIS_FREQS, 1),
                                pltpu.roll(y, AXIS_FREQS, 1))
            return y * cos + partner * sin

        for h in range(N_HEADS):
            col = h * HEAD_DIM
            q_ref[0, rows, col:col + HEAD_DIM] = (head(col, qg) * scale).astype(BF16)
        for h in range(N_KV_HEADS):
            col = h * HEAD_DIM
            kt_ref[0, col:col + HEAD_DIM, rows] = head(ATTN_DIM + col, kg).T.astype(BF16)
        v_ref[0, rows, :] = qkv[:, ATTN_DIM + KV_DIM:].astype(BF16)


def _qkv(x, gain, w_qkv, q_gain, k_gain, cos, sin, layer):
    b, s, d = x.shape
    tm = _pick(s, 512)
    n = w_qkv.shape[2]
    return pl.pallas_call(
        functools.partial(_qkv_kernel, scale=HEAD_DIM ** -0.5 * LOG2_E),
        name="qkv",
        grid=(b, s // tm),
        in_specs=[
            pl.BlockSpec((1, tm, d), lambda bi, i: (bi, i, 0)),
            _resident((1, d)),
            _layer_resident((d, n), layer),
            _resident((1, HEAD_DIM)),
            _resident((1, HEAD_DIM)),
            pl.BlockSpec((tm, HEAD_DIM), lambda bi, i: (i, 0)),
            pl.BlockSpec((tm, HEAD_DIM), lambda bi, i: (i, 0)),
        ],
        out_specs=[
            pl.BlockSpec((1, tm, ATTN_DIM), lambda bi, i: (bi, i, 0)),
            pl.BlockSpec((1, KV_DIM, tm), lambda bi, i: (bi, 0, i)),
            pl.BlockSpec((1, tm, KV_DIM), lambda bi, i: (bi, i, 0)),
        ],
        out_shape=[
            jax.ShapeDtypeStruct((b, s, ATTN_DIM), BF16),
            jax.ShapeDtypeStruct((b, KV_DIM, s), BF16),
            jax.ShapeDtypeStruct((b, s, KV_DIM), BF16),
        ],
        compiler_params=_params(2),
    )(x, gain.reshape(1, d), w_qkv, q_gain.reshape(1, HEAD_DIM), k_gain.reshape(1, HEAD_DIM), cos, sin)


def _attn_kernel(q_ref, kt_ref, v_ref, o_ref, *, tq, tk, n_chunks):
    q = jnp.concatenate(
        [q_ref[0, :, g * HEAD_DIM:(g + 1) * HEAD_DIM] for g in range(GQA_GROUP)], axis=0)
    m = l = acc = None
    for c in range(n_chunks):
        s = _dot(q, kt_ref[0, :, c * tk:(c + 1) * tk])
        m_c = jnp.max(s, axis=-1, keepdims=True)
        if c == 0:
            m_new = m_c
            p = jnp.exp2(s - m_new)
            l = jnp.sum(p, axis=-1, keepdims=True)
            acc = _dot(p.astype(BF16), v_ref[0, c * tk:(c + 1) * tk, :])
        else:
            m_new = jnp.maximum(m, m_c)
            alpha = jnp.exp2(m - m_new)
            p = jnp.exp2(s - m_new)
            l = alpha * l + jnp.sum(p, axis=-1, keepdims=True)
            acc = alpha * acc + _dot(p.astype(BF16), v_ref[0, c * tk:(c + 1) * tk, :])
        m = m_new
    out = acc / l
    for g in range(GQA_GROUP):
        o_ref[0, :, g * HEAD_DIM:(g + 1) * HEAD_DIM] = out[g * tq:(g + 1) * tq].astype(BF16)


def _attention(q, kt, v):
    b, s, _ = q.shape
    tq = _pick(s, ATTN_TQ)
    tk = _pick(s, ATTN_TK)
    gw = GQA_GROUP * HEAD_DIM
    return pl.pallas_call(
        functools.partial(_attn_kernel, tq=tq, tk=tk, n_chunks=s // tk),
        name="attn",
        grid=(b, N_KV_HEADS, s // tq),
        in_specs=[
            pl.BlockSpec((1, tq, gw), lambda bi, h, i: (bi, i, h)),
            pl.BlockSpec((1, HEAD_DIM, s), lambda bi, h, i: (bi, h, 0)),
            pl.BlockSpec((1, s, HEAD_DIM), lambda bi, h, i: (bi, 0, h)),
        ],
        out_specs=pl.BlockSpec((1, tq, gw), lambda bi, h, i: (bi, i, h)),
        out_shape=jax.ShapeDtypeStruct((b, s, ATTN_DIM), BF16),
        compiler_params=_params(3),
    )(q, kt, v)


def _oproj_kernel(a_ref, x_ref, w_ref, o_ref):
    o_ref[0] = x_ref[0] + _dot(a_ref[0], w_ref[...])


def _oproj(a, x, w_o, layer):
    b, s, d = x.shape
    tm = _pick(s, 512)
    return pl.pallas_call(
        _oproj_kernel,
        name="oproj",
        grid=(b, s // tm),
        in_specs=[
            pl.BlockSpec((1, tm, ATTN_DIM), lambda bi, i: (bi, i, 0)),
            pl.BlockSpec((1, tm, d), lambda bi, i: (bi, i, 0)),
            _layer_resident((ATTN_DIM, d), layer),
        ],
        out_specs=pl.BlockSpec((1, tm, d), lambda bi, i: (bi, i, 0)),
        out_shape=jax.ShapeDtypeStruct((b, s, d), F32),
        compiler_params=_params(2),
    )(a, x, w_o)


def _pw1_kernel(x_ref, gain_ref, w_ref, b_ref, u_ref, *, d):
    ts = x_ref.shape[1] // ROW_SUB_TILES
    for c in range(ROW_SUB_TILES):
        rows = slice(c * ts, (c + 1) * ts)
        hn = _rms_norm(x_ref[0, rows, :], gain_ref[...]).astype(BF16)
        u = _dot(hn, w_ref[...]) + b_ref[...]
        u_ref[0, rows, :] = u[:, :d] * _sigmoid(u[:, d:])


def _pw1(x, gain, w_pw1, b_pw1, layer):
    b, s, d = x.shape
    tm = _pick(s, 512)
    return pl.pallas_call(
        functools.partial(_pw1_kernel, d=d),
        name="pw1",
        grid=(b, s // tm),
        in_specs=[
            pl.BlockSpec((1, tm, d), lambda bi, i: (bi, i, 0)),
            _resident((1, d)),
            _layer_resident((d, 2 * d), layer),
            _resident((1, 2 * d)),
        ],
        out_specs=pl.BlockSpec((1, tm, d), lambda bi, i: (bi, i, 0)),
        out_shape=jax.ShapeDtypeStruct((b, s, d), F32),
        compiler_params=_params(2),
    )(x, gain.reshape(1, d), w_pw1, b_pw1.reshape(1, 2 * d))


def _convpw2_kernel(up_ref, u_ref, un_ref, x_ref, wdw_ref, bdw_ref, lng_ref, lnb_ref,
                    w2_ref, b2_ref, o_ref, ext_ref, conv_ref, *, tm, cw, n_seq_tiles):
    i = pl.program_id(1)
    rows = tm + 2 * CONV_HALO
    d = conv_ref.shape[1]
    ts = tm // ROW_SUB_TILES
    sub_rows = ts + 2 * CONV_HALO
    ext_ref[0:CONV_HALO, :] = jnp.where(i > 0, up_ref[0], 0.0)
    ext_ref[CONV_HALO:CONV_HALO + tm, :] = u_ref[0]
    ext_ref[CONV_HALO + tm:rows, :] = jnp.where(i < n_seq_tiles - 1, un_ref[0], 0.0)

    base = CONV_HALO - CONV_K // 2
    for sub in range(ROW_SUB_TILES):
        r0 = sub * ts
        out_rows = slice(r0, r0 + ts)
        for c in range(d // cw):
            cols = slice(c * cw, (c + 1) * cw)
            ue = ext_ref[r0:r0 + sub_rows, cols]
            acc = None
            for r in range(V7X_SUBLANES):
                ur = ue if r == 0 else pltpu.roll(ue, sub_rows - r, 0)
                for a in range(2 * CONV_HALO // V7X_SUBLANES):
                    k = a * V7X_SUBLANES + r - base
                    if 0 <= k < CONV_K:
                        term = ur[a * V7X_SUBLANES:a * V7X_SUBLANES + ts] * wdw_ref[k:k + 1, cols]
                        acc = term if acc is None else acc + term
            conv_ref[out_rows, cols] = acc + bdw_ref[:, cols]

        y = conv_ref[out_rows, :]
        mu = jnp.mean(y, axis=-1, keepdims=True)
        yc = y - mu
        var = jnp.mean(yc * yc, axis=-1, keepdims=True)
        y = yc * lax.rsqrt(var + EPS) * lng_ref[...] + lnb_ref[...]
        y = y * _sigmoid(y)
        o_ref[0, out_rows, :] = x_ref[0, out_rows, :] + (_dot(y.astype(BF16), w2_ref[...]) + b2_ref[...])


def _convpw2(u, x, w_dw, b_dw, ln_g, ln_b, w_pw2, b_pw2, layer):
    b, s, d = x.shape
    tm = _pick(s, 512)
    cw = _pick(d, 512)
    n_seq = s // tm
    hb = tm // CONV_HALO
    last_hb = s // CONV_HALO - 1
    row = lambda a: a.reshape(1, d)
    return pl.pallas_call(
        functools.partial(_convpw2_kernel, tm=tm, cw=cw, n_seq_tiles=n_seq),
        name="convpw2",
        grid=(b, n_seq),
        in_specs=[
            pl.BlockSpec((1, CONV_HALO, d), lambda bi, i: (bi, jnp.maximum(i * hb - 1, 0), 0)),
            pl.BlockSpec((1, tm, d), lambda bi, i: (bi, i, 0)),
            pl.BlockSpec((1, CONV_HALO, d), lambda bi, i: (bi, jnp.minimum((i + 1) * hb, last_hb), 0)),
            pl.BlockSpec((1, tm, d), lambda bi, i: (bi, i, 0)),
            _resident((CONV_K, d)),
            _resident((1, d)),
            _resident((1, d)),
            _resident((1, d)),
            _layer_resident((d, d), layer),
            _resident((1, d)),
        ],
        out_specs=pl.BlockSpec((1, tm, d), lambda bi, i: (bi, i, 0)),
        out_shape=jax.ShapeDtypeStruct((b, s, d), F32),
        scratch_shapes=[pltpu.VMEM((tm + 2 * CONV_HALO, d), F32), pltpu.VMEM((tm, d), F32)],
        compiler_params=_params(2),
    )(u, u, u, x, w_dw, row(b_dw), row(ln_g), row(ln_b), w_pw2, row(b_pw2))


def _trunk(x, p):
    depth = p["ffn_norm"].shape[0]
    cos, sin = _rope_tables(x.shape[1])
    for i in range(depth):
        j = i // 2
        if i % 2 == 0:
            q, kt, v = _qkv(x, p["attn_norm"][j], p["attn_w_qkv"], p["attn_q_norm"][j],
                            p["attn_k_norm"][j], cos, sin, j)
            x = _oproj(_attention(q, kt, v), x, p["attn_w_o"], j)
        else:
            u = _pw1(x, p["conv_norm"][j], p["conv_w_pw1"], p["conv_b_pw1"][j], j)
            x = _convpw2(u, x, p["conv_w_dw"][j], p["conv_b_dw"][j], p["conv_ln_g"][j],
                         p["conv_ln_b"][j], p["conv_w_pw2"], p["conv_b_pw2"][j], j)
        x = _ffn(x, p["ffn_norm"][i], p["ffn_w_up"], p["ffn_w_dw"][i], p["ffn_b_dw"][i],
                 p["ffn_w_down"], i)
    return x


def kernel(x_prompt, x_sample, attn_norm, attn_w_qkv, attn_q_norm, attn_k_norm, attn_w_o,
           conv_norm, conv_w_pw1, conv_b_pw1, conv_w_dw, conv_b_dw, conv_ln_g, conv_ln_b,
           conv_w_pw2, conv_b_pw2, ffn_norm, ffn_w_up, ffn_w_dw, ffn_b_dw, ffn_w_down):
    p = dict(
        attn_norm=attn_norm, attn_w_qkv=attn_w_qkv.astype(BF16), attn_q_norm=attn_q_norm,
        attn_k_norm=attn_k_norm, attn_w_o=attn_w_o.astype(BF16),
        conv_norm=conv_norm, conv_w_pw1=conv_w_pw1.astype(BF16), conv_b_pw1=conv_b_pw1,
        conv_w_dw=conv_w_dw, conv_b_dw=conv_b_dw, conv_ln_g=conv_ln_g, conv_ln_b=conv_ln_b,
        conv_w_pw2=conv_w_pw2.astype(BF16), conv_b_pw2=conv_b_pw2,
        ffn_norm=ffn_norm, ffn_w_up=ffn_w_up.astype(BF16), ffn_w_dw=ffn_w_dw, ffn_b_dw=ffn_b_dw,
        ffn_w_down=ffn_w_down.astype(BF16))
    return _trunk(x_prompt, p), _trunk(x_sample, p)
```

```python
import functools

import jax
import jax.numpy as jnp
from jax import lax
from jax.experimental import pallas as pl
from jax.experimental.pallas import tpu as pltpu

N_HEADS = 16
N_KV_HEADS = 4
HEAD_DIM = 128
GQA_GROUP = N_HEADS // N_KV_HEADS
ATTN_DIM = N_HEADS * HEAD_DIM
KV_DIM = N_KV_HEADS * HEAD_DIM
ROPE_THETA = 10000.0
AXIS_DIM = HEAD_DIM // 2
AXIS_FREQS = AXIS_DIM // 2
GRID_W = 64
CONV_K = 31
FFN_CONV_K = 3
EPS = 1e-6

V7X_SUBLANES = 8
V7X_LANES = 128
V7X_VMEM_LIMIT_BYTES = 56 * 1024 * 1024

FFN_HALO = V7X_SUBLANES
CONV_HALO = 2 * V7X_SUBLANES

ATTN_TQ = 256
ATTN_TK = 2048
ROW_SUB_TILES = 2
LOG2_E = 1.4426950408889634

BF16 = jnp.bfloat16
F32 = jnp.float32


def _pick(total, preferred):
    if total <= preferred:
        return total
    t = preferred
    while t >= V7X_LANES:
        if total % t == 0:
            return t
        t -= V7X_LANES
    return total


def _params(n_grid):
    return pltpu.CompilerParams(
        dimension_semantics=("arbitrary",) * n_grid,
        vmem_limit_bytes=V7X_VMEM_LIMIT_BYTES)


def _resident(shape):
    return pl.BlockSpec(shape, lambda *_: (0,) * len(shape), pipeline_mode=pl.Buffered(1))


def _layer_resident(shape, layer):
    return pl.BlockSpec((None,) + shape, lambda *_: (layer,) + (0,) * len(shape),
                        pipeline_mode=pl.Buffered(1))


def _rms_norm(x, gain):
    y = x * lax.rsqrt(jnp.mean(x * x, axis=-1, keepdims=True) + EPS)
    return y * gain


def _sigmoid(x):
    return 0.5 * jnp.tanh(0.5 * x) + 0.5


def _dot(a, b):
    return jnp.dot(a, b, preferred_element_type=F32)


def _ffn_kernel(xp_ref, x_ref, xn_ref, gain_ref, wg_ref, wv_ref, dwg_ref, dwv_ref,
                bg_ref, bv_ref, wd_ref, o_ref, hn_ref, *, tm, n_seq_tiles):
    i = pl.program_id(1)
    j = pl.program_id(2)
    rows = tm + 2 * FFN_HALO

    @pl.when(j == 0)
    def _():
        gain = gain_ref[...]
        x = x_ref[0]
        hn_ref[FFN_HALO:FFN_HALO + tm, :] = _rms_norm(x, gain).astype(BF16)
        prev = jnp.where(i > 0, _rms_norm(xp_ref[0], gain), 0.0)
        hn_ref[0:FFN_HALO, :] = prev.astype(BF16)
        nxt = jnp.where(i < n_seq_tiles - 1, _rms_norm(xn_ref[0], gain), 0.0)
        hn_ref[FFN_HALO + tm:rows, :] = nxt.astype(BF16)
        o_ref[0] = x

    hn = hn_ref[...]

    def conv(w_ref, dw_ref, b_ref):
        u = _dot(hn, w_ref[...])
        dw = dw_ref[...]
        lo = pltpu.roll(u, 1, 0)
        hi = pltpu.roll(u, rows - 1, 0)
        s = slice(FFN_HALO, FFN_HALO + tm)
        return lo[s] * dw[0:1] + u[s] * dw[1:2] + hi[s] * dw[2:3] + b_ref[...]

    g = conv(wg_ref, dwg_ref, bg_ref)
    v = conv(wv_ref, dwv_ref, bv_ref)
    act = (g * _sigmoid(g)) * v
    o_ref[0] += _dot(act.astype(BF16), wd_ref[...])


def _ffn(x, gain, w_up, w_dw, b_dw, w_down, layer):
    b, s, d = x.shape
    f = w_down.shape[1]
    tm = _pick(s, 512)
    tf = _pick(f, 512)
    n_seq, n_f = s // tm, f // tf
    hb = tm // FFN_HALO
    last_hb = s // FFN_HALO - 1
    gain = gain.reshape(1, d)
    b_dw = b_dw.reshape(1, 2 * f)

    return pl.pallas_call(
        functools.partial(_ffn_kernel, tm=tm, n_seq_tiles=n_seq),
        name="ffn",
        grid=(b, n_seq, n_f),
        in_specs=[
            pl.BlockSpec((1, FFN_HALO, d), lambda bi, i, j: (bi, jnp.maximum(i * hb - 1, 0), 0)),
            pl.BlockSpec((1, tm, d), lambda bi, i, j: (bi, i, 0)),
            pl.BlockSpec((1, FFN_HALO, d), lambda bi, i, j: (bi, jnp.minimum((i + 1) * hb, last_hb), 0)),
            pl.BlockSpec((1, d), lambda bi, i, j: (0, 0)),
            pl.BlockSpec((None, d, tf), lambda bi, i, j: (layer, 0, j)),
            pl.BlockSpec((None, d, tf), lambda bi, i, j: (layer, 0, n_f + j)),
            pl.BlockSpec((FFN_CONV_K, tf), lambda bi, i, j: (0, j)),
            pl.BlockSpec((FFN_CONV_K, tf), lambda bi, i, j: (0, n_f + j)),
            pl.BlockSpec((1, tf), lambda bi, i, j: (0, j)),
            pl.BlockSpec((1, tf), lambda bi, i, j: (0, n_f + j)),
            pl.BlockSpec((None, tf, d), lambda bi, i, j: (layer, j, 0)),
        ],
        out_specs=pl.BlockSpec((1, tm, d), lambda bi, i, j: (bi, i, 0)),
        out_shape=jax.ShapeDtypeStruct((b, s, d), F32),
        scratch_shapes=[pltpu.VMEM((tm + 2 * FFN_HALO, d), BF16)],
        compiler_params=_params(3),
    )(x, x, x, gain, w_up, w_up, w_dw, w_dw, b_dw, b_dw, w_down)


def _rope_tables(seq_len):
    rows = seq_len // GRID_W
    row_pos = jnp.repeat(jnp.arange(rows, dtype=F32), GRID_W)
    col_pos = jnp.tile(jnp.arange(GRID_W, dtype=F32), rows)
    freqs = ROPE_THETA ** (-jnp.arange(AXIS_FREQS, dtype=F32) * 2.0 / AXIS_DIM)
    ang_r = row_pos[:, None] * freqs[None, :]
    ang_c = col_pos[:, None] * freqs[None, :]
    cr, sr, cc, sc = jnp.cos(ang_r), jnp.sin(ang_r), jnp.cos(ang_c), jnp.sin(ang_c)
    cos = jnp.concatenate([cr, cr, cc, cc], axis=-1)
    sin = jnp.concatenate([-sr, sr, -sc, sc], axis=-1)
    return cos, sin


def _qkv_kernel(x_ref, gain_ref, w_ref, qg_ref, kg_ref, cos_ref, sin_ref,
                q_ref, kt_ref, v_ref, *, scale):
    ts = x_ref.shape[1] // ROW_SUB_TILES
    qg = qg_ref[...]
    kg = kg_ref[...]
    lane = lax.broadcasted_iota(jnp.int32, (ts, HEAD_DIM), 1)
    first = (lane % (2 * AXIS_FREQS)) < AXIS_FREQS
    for c in range(ROW_SUB_TILES):
        rows = slice(c * ts, (c + 1) * ts)
        hn = _rms_norm(x_ref[0, rows, :], gain_ref[...]).astype(BF16)
        qkv = _dot(hn, w_ref[...])
        cos = cos_ref[rows, :]
        sin = sin_ref[rows, :]

        def head(col, gain):
            y = _rms_norm(qkv[:, col:col + HEAD_DIM], gain)
            partner = jnp.where(first,
                                pltpu.roll(y, HEAD_DIM - AXIS_FREQS, 1),
                                pltpu.roll(y, AXIS_FREQS, 1))
            return y * cos + partner * sin

        for h in range(N_HEADS):
            col = h * HEAD_DIM
            q_ref[0, rows, col:col + HEAD_DIM] = (head(col, qg) * scale).astype(BF16)
        for h in range(N_KV_HEADS):
            col = h * HEAD_DIM
            kt_ref[0, col:col + HEAD_DIM, rows] = head(ATTN_DIM + col, kg).T.astype(BF16)
        v_ref[0, rows, :] = qkv[:, ATTN_DIM + KV_DIM:].astype(BF16)


def _qkv(x, gain, w_qkv, q_gain, k_gain, cos, sin, layer):
    b, s, d = x.shape
    tm = _pick(s, 512)
    n = w_qkv.shape[2]
    return pl.pallas_call(
        functools.partial(_qkv_kernel, scale=HEAD_DIM ** -0.5 * LOG2_E),
        name="qkv",
        grid=(b, s // tm),
        in_specs=[
            pl.BlockSpec((1, tm, d), lambda bi, i: (bi, i, 0)),
            _resident((1, d)),
            _layer_resident((d, n), layer),
            _resident((1, HEAD_DIM)),
            _resident((1, HEAD_DIM)),
            pl.BlockSpec((tm, HEAD_DIM), lambda bi, i: (i, 0)),
            pl.BlockSpec((tm, HEAD_DIM), lambda bi, i: (i, 0)),
        ],
        out_specs=[
            pl.BlockSpec((1, tm, ATTN_DIM), lambda bi, i: (bi, i, 0)),
            pl.BlockSpec((1, KV_DIM, tm), lambda bi, i: (bi, 0, i)),
            pl.BlockSpec((1, tm, KV_DIM), lambda bi, i: (bi, i, 0)),
        ],
        out_shape=[
            jax.ShapeDtypeStruct((b, s, ATTN_DIM), BF16),
            jax.ShapeDtypeStruct((b, KV_DIM, s), BF16),
            jax.ShapeDtypeStruct((b, s, KV_DIM), BF16),
        ],
        compiler_params=_params(2),
    )(x, gain.reshape(1, d), w_qkv, q_gain.reshape(1, HEAD_DIM), k_gain.reshape(1, HEAD_DIM), cos, sin)


def _attn_kernel(q_ref, kt_ref, v_ref, o_ref, *, tq, tk, n_chunks):
    q = jnp.concatenate(
        [q_ref[0, :, g * HEAD_DIM:(g + 1) * HEAD_DIM] for g in range(GQA_GROUP)], axis=0)
    m = l = acc = None
    for c in range(n_chunks):
        s = _dot(q, kt_ref[0, :, c * tk:(c + 1) * tk])
        m_c = jnp.max(s, axis=-1, keepdims=True)
        if c == 0:
            m_new = m_c
            p = jnp.exp2(s - m_new)
            l = jnp.sum(p, axis=-1, keepdims=True)
            acc = _dot(p.astype(BF16), v_ref[0, c * tk:(c + 1) * tk, :])
        else:
            m_new = jnp.maximum(m, m_c)
            alpha = jnp.exp2(m - m_new)
            p = jnp.exp2(s - m_new)
            l = alpha * l + jnp.sum(p, axis=-1, keepdims=True)
            acc = alpha * acc + _dot(p.astype(BF16), v_ref[0, c * tk:(c + 1) * tk, :])
        m = m_new
    out = acc / l
    for g in range(GQA_GROUP):
        o_ref[0, :, g * HEAD_DIM:(g + 1) * HEAD_DIM] = out[g * tq:(g + 1) * tq].astype(BF16)


def _attention(q, kt, v):
    b, s, _ = q.shape
    tq = _pick(s, ATTN_TQ)
    tk = _pick(s, ATTN_TK)
    gw = GQA_GROUP * HEAD_DIM
    return pl.pallas_call(
        functools.partial(_attn_kernel, tq=tq, tk=tk, n_chunks=s // tk),
        name="attn",
        grid=(b, N_KV_HEADS, s // tq),
        in_specs=[
            pl.BlockSpec((1, tq, gw), lambda bi, h, i: (bi, i, h)),
            pl.BlockSpec((1, HEAD_DIM, s), lambda bi, h, i: (bi, h, 0)),
            pl.BlockSpec((1, s, HEAD_DIM), lambda bi, h, i: (bi, 0, h)),
        ],
        out_specs=pl.BlockSpec((1, tq, gw), lambda bi, h, i: (bi, i, h)),
        out_shape=jax.ShapeDtypeStruct((b, s, ATTN_DIM), BF16),
        compiler_params=_params(3),
    )(q, kt, v)


def _oproj_kernel(a_ref, x_ref, w_ref, o_ref):
    o_ref[0] = x_ref[0] + _dot(a_ref[0], w_ref[...])


def _oproj(a, x, w_o, layer):
    b, s, d = x.shape
    tm = _pick(s, 512)
    return pl.pallas_call(
        _oproj_kernel,
        name="oproj",
        grid=(b, s // tm),
        in_specs=[
            pl.BlockSpec((1, tm, ATTN_DIM), lambda bi, i: (bi, i, 0)),
            pl.BlockSpec((1, tm, d), lambda bi, i: (bi, i, 0)),
            _layer_resident((ATTN_DIM, d), layer),
        ],
        out_specs=pl.BlockSpec((1, tm, d), lambda bi, i: (bi, i, 0)),
        out_shape=jax.ShapeDtypeStruct((b, s, d), F32),
        compiler_params=_params(2),
    )(a, x, w_o)


def _pw1_kernel(x_ref, gain_ref, w_ref, b_ref, u_ref, *, d):
    ts = x_ref.shape[1] // ROW_SUB_TILES
    for c in range(ROW_SUB_TILES):
        rows = slice(c * ts, (c + 1) * ts)
        hn = _rms_norm(x_ref[0, rows, :], gain_ref[...]).astype(BF16)
        u = _dot(hn, w_ref[...]) + b_ref[...]
        u_ref[0, rows, :] = u[:, :d] * _sigmoid(u[:, d:])


def _pw1(x, gain, w_pw1, b_pw1, layer):
    b, s, d = x.shape
    tm = _pick(s, 512)
    return pl.pallas_call(
        functools.partial(_pw1_kernel, d=d),
        name="pw1",
        grid=(b, s // tm),
        in_specs=[
            pl.BlockSpec((1, tm, d), lambda bi, i: (bi, i, 0)),
            _resident((1, d)),
            _layer_resident((d, 2 * d), layer),
            _resident((1, 2 * d)),
        ],
        out_specs=pl.BlockSpec((1, tm, d), lambda bi, i: (bi, i, 0)),
        out_shape=jax.ShapeDtypeStruct((b, s, d), F32),
        compiler_params=_params(2),
    )(x, gain.reshape(1, d), w_pw1, b_pw1.reshape(1, 2 * d))


def _convpw2_kernel(up_ref, u_ref, un_ref, x_ref, wdw_ref, bdw_ref, lng_ref, lnb_ref,
                    w2_ref, b2_ref, o_ref, ext_ref, conv_ref, *, tm, cw, n_seq_tiles):
    i = pl.program_id(1)
    rows = tm + 2 * CONV_HALO
    d = conv_ref.shape[1]
    ext_ref[0:CONV_HALO, :] = jnp.where(i > 0, up_ref[0], 0.0)
    ext_ref[CONV_HALO:CONV_HALO + tm, :] = u_ref[0]
    ext_ref[CONV_HALO + tm:rows, :] = jnp.where(i < n_seq_tiles - 1, un_ref[0], 0.0)

    base = CONV_HALO - CONV_K // 2
    for c in range(d // cw):
        cols = slice(c * cw, (c + 1) * cw)
        ue = ext_ref[:, cols]
        acc = None
        for r in range(V7X_SUBLANES):
            ur = ue if r == 0 else pltpu.roll(ue, rows - r, 0)
            for a in range(2 * CONV_HALO // V7X_SUBLANES):
                k = a * V7X_SUBLANES + r - base
                if 0 <= k < CONV_K:
                    term = ur[a * V7X_SUBLANES:a * V7X_SUBLANES + tm] * wdw_ref[k:k + 1, cols]
                    acc = term if acc is None else acc + term
        conv_ref[:, cols] = acc + bdw_ref[:, cols]

    y = conv_ref[...]
    mu = jnp.mean(y, axis=-1, keepdims=True)
    yc = y - mu
    var = jnp.mean(yc * yc, axis=-1, keepdims=True)
    y = yc * lax.rsqrt(var + EPS) * lng_ref[...] + lnb_ref[...]
    y = y * _sigmoid(y)
    o_ref[0] = x_ref[0] + (_dot(y.astype(BF16), w2_ref[...]) + b2_ref[...])


def _convpw2(u, x, w_dw, b_dw, ln_g, ln_b, w_pw2, b_pw2, layer):
    b, s, d = x.shape
    tm = _pick(s, 512)
    cw = _pick(d, 512)
    n_seq = s // tm
    hb = tm // CONV_HALO
    last_hb = s // CONV_HALO - 1
    row = lambda a: a.reshape(1, d)
    return pl.pallas_call(
        functools.partial(_convpw2_kernel, tm=tm, cw=cw, n_seq_tiles=n_seq),
        name="convpw2",
        grid=(b, n_seq),
        in_specs=[
            pl.BlockSpec((1, CONV_HALO, d), lambda bi, i: (bi, jnp.maximum(i * hb - 1, 0), 0)),
            pl.BlockSpec((1, tm, d), lambda bi, i: (bi, i, 0)),
            pl.BlockSpec((1, CONV_HALO, d), lambda bi, i: (bi, jnp.minimum((i + 1) * hb, last_hb), 0)),
            pl.BlockSpec((1, tm, d), lambda bi, i: (bi, i, 0)),
            _resident((CONV_K, d)),
            _resident((1, d)),
            _resident((1, d)),
            _resident((1, d)),
            _layer_resident((d, d), layer),
            _resident((1, d)),
        ],
        out_specs=pl.BlockSpec((1, tm, d), lambda bi, i: (bi, i, 0)),
        out_shape=jax.ShapeDtypeStruct((b, s, d), F32),
        scratch_shapes=[pltpu.VMEM((tm + 2 * CONV_HALO, d), F32), pltpu.VMEM((tm, d), F32)],
        compiler_params=_params(2),
    )(u, u, u, x, w_dw, row(b_dw), row(ln_g), row(ln_b), w_pw2, row(b_pw2))


def _trunk(x, p):
    depth = p["ffn_norm"].shape[0]
    cos, sin = _rope_tables(x.shape[1])
    for i in range(depth):
        j = i // 2
        if i % 2 == 0:
            q, kt, v = _qkv(x, p["attn_norm"][j], p["attn_w_qkv"], p["attn_q_norm"][j],
                            p["attn_k_norm"][j], cos, sin, j)
            x = _oproj(_attention(q, kt, v), x, p["attn_w_o"], j)
        else:
            u = _pw1(x, p["conv_norm"][j], p["conv_w_pw1"], p["conv_b_pw1"][j], j)
            x = _convpw2(u, x, p["conv_w_dw"][j], p["conv_b_dw"][j], p["conv_ln_g"][j],
                         p["conv_ln_b"][j], p["conv_w_pw2"], p["conv_b_pw2"][j], j)
        x = _ffn(x, p["ffn_norm"][i], p["ffn_w_up"], p["ffn_w_dw"][i], p["ffn_b_dw"][i],
                 p["ffn_w_down"], i)
    return x


def kernel(x_prompt, x_sample, attn_norm, attn_w_qkv, attn_q_norm, attn_k_norm, attn_w_o,
           conv_norm, conv_w_pw1, conv_b_pw1, conv_w_dw, conv_b_dw, conv_ln_g, conv_ln_b,
           conv_w_pw2, conv_b_pw2, ffn_norm, ffn_w_up, ffn_w_dw, ffn_b_dw, ffn_w_down):
    p = dict(
        attn_norm=attn_norm, attn_w_qkv=attn_w_qkv.astype(BF16), attn_q_norm=attn_q_norm,
        attn_k_norm=attn_k_norm, attn_w_o=attn_w_o.astype(BF16),
        conv_norm=conv_norm, conv_w_pw1=conv_w_pw1.astype(BF16), conv_b_pw1=conv_b_pw1,
        conv_w_dw=conv_w_dw, conv_b_dw=conv_b_dw, conv_ln_g=conv_ln_g, conv_ln_b=conv_ln_b,
        conv_w_pw2=conv_w_pw2.astype(BF16), conv_b_pw2=conv_b_pw2,
        ffn_norm=ffn_norm, ffn_w_up=ffn_w_up.astype(BF16), ffn_w_dw=ffn_w_dw, ffn_b_dw=ffn_b_dw,
        ffn_w_down=ffn_w_down.astype(BF16))
    return _trunk(x_prompt, p), _trunk(x_sample, p)
```

```python
import functools

import jax
import jax.numpy as jnp
from jax import lax
from jax.experimental import pallas as pl
from jax.experimental.pallas import tpu as pltpu

N_HEADS = 16
N_KV_HEADS = 4
HEAD_DIM = 128
GQA_GROUP = N_HEADS // N_KV_HEADS
ATTN_DIM = N_HEADS * HEAD_DIM
KV_DIM = N_KV_HEADS * HEAD_DIM
ROPE_THETA = 10000.0
AXIS_DIM = HEAD_DIM // 2
AXIS_FREQS = AXIS_DIM // 2
GRID_W = 64
CONV_K = 31
FFN_CONV_K = 3
EPS = 1e-6

V7X_SUBLANES = 8
V7X_LANES = 128
V7X_VMEM_LIMIT_BYTES = 56 * 1024 * 1024

FFN_HALO = V7X_SUBLANES
CONV_HALO = 2 * V7X_SUBLANES

ATTN_TQ = 256
ATTN_TK = 2048
ATTN_MIN_CHUNKS = 4
ROW_SUB_TILES = 2
LOG2_E = 1.4426950408889634

BF16 = jnp.bfloat16
F32 = jnp.float32


def _pick(total, preferred):
    if total <= preferred:
        return total
    t = preferred
    while t >= V7X_LANES:
        if total % t == 0:
            return t
        t -= V7X_LANES
    return total


def _params(n_grid):
    return pltpu.CompilerParams(
        dimension_semantics=("arbitrary",) * n_grid,
        vmem_limit_bytes=V7X_VMEM_LIMIT_BYTES)


def _resident(shape):
    return pl.BlockSpec(shape, lambda *_: (0,) * len(shape), pipeline_mode=pl.Buffered(1))


def _layer_resident(shape, layer):
    return pl.BlockSpec((None,) + shape, lambda *_: (layer,) + (0,) * len(shape),
                        pipeline_mode=pl.Buffered(1))


def _rms_norm(x, gain):
    y = x * lax.rsqrt(jnp.mean(x * x, axis=-1, keepdims=True) + EPS)
    return y * gain


def _sigmoid(x):
    return 0.5 * jnp.tanh(0.5 * x) + 0.5


def _dot(a, b):
    return jnp.dot(a, b, preferred_element_type=F32)


def _ffn_kernel(x_ref, xnext_ref, xnn_ref, gain_ref, wg_ref, wv_ref, dwg_ref, dwv_ref,
                bg_ref, bv_ref, wd_ref, o_ref, hn_ref, hnn_ref, *, tm, n_seq_tiles, n_f):
    i = pl.program_id(1)
    j = pl.program_id(2)
    rows = tm + 2 * FFN_HALO

    def norm_into(dst_ref, prev_rows, main, next_rows, has_next):
        gain = gain_ref[...]
        dst_ref[FFN_HALO:FFN_HALO + tm, :] = _rms_norm(main, gain).astype(BF16)
        if prev_rows is None:
            dst_ref[0:FFN_HALO, :] = jnp.zeros((FFN_HALO, main.shape[1]), BF16)
        else:
            dst_ref[0:FFN_HALO, :] = _rms_norm(prev_rows, gain).astype(BF16)
        dst_ref[FFN_HALO + tm:rows, :] = jnp.where(has_next, _rms_norm(next_rows, gain), 0.0).astype(BF16)

    def step():
        hn = hn_ref[...]

        def conv(w_ref, dw_ref, b_ref):
            u = _dot(hn, w_ref[...])
            dw = dw_ref[...]
            lo = pltpu.roll(u, 1, 0)
            hi = pltpu.roll(u, rows - 1, 0)
            s = slice(FFN_HALO, FFN_HALO + tm)
            return lo[s] * dw[0:1] + u[s] * dw[1:2] + hi[s] * dw[2:3] + b_ref[...]

        g = conv(wg_ref, dwg_ref, bg_ref)
        v = conv(wv_ref, dwv_ref, bv_ref)
        act = (g * _sigmoid(g)) * v
        o_ref[0] += _dot(act.astype(BF16), wd_ref[...])

    @pl.when(j == 0)
    def _():
        @pl.when(i == 0)
        def _():
            norm_into(hn_ref, None, x_ref[0], xnext_ref[0, 0:FFN_HALO, :], n_seq_tiles > 1)

        @pl.when(i > 0)
        def _():
            hn_ref[...] = hnn_ref[...]

        o_ref[0] = x_ref[0]

    @pl.when(j < n_f - 1)
    def _():
        step()

    @pl.when(j == n_f - 1)
    def _():
        norm_into(hnn_ref, x_ref[0, tm - FFN_HALO:tm, :], xnext_ref[0], xnn_ref[0], i + 1 < n_seq_tiles - 1)
        step()


def _ffn(x, gain, w_up, w_dw, b_dw, w_down, layer):
    b, s, d = x.shape
    f = w_down.shape[1]
    tm = _pick(s, 512)
    tf = _pick(f, 512)
    n_seq, n_f = s // tm, f // tf
    hb = tm // FFN_HALO
    last_hb = s // FFN_HALO - 1
    gain = gain.reshape(1, d)
    b_dw = b_dw.reshape(1, 2 * f)

    return pl.pallas_call(
        functools.partial(_ffn_kernel, tm=tm, n_seq_tiles=n_seq, n_f=n_f),
        name="ffn",
        grid=(b, n_seq, n_f),
        in_specs=[
            pl.BlockSpec((1, tm, d), lambda bi, i, j: (bi, i, 0)),
            pl.BlockSpec((1, tm, d), lambda bi, i, j: (bi, jnp.minimum(i + 1, n_seq - 1), 0)),
            pl.BlockSpec((1, FFN_HALO, d), lambda bi, i, j: (bi, jnp.minimum((i + 2) * hb, last_hb), 0)),
            pl.BlockSpec((1, d), lambda bi, i, j: (0, 0)),
            pl.BlockSpec((None, d, tf), lambda bi, i, j: (layer, 0, j)),
            pl.BlockSpec((None, d, tf), lambda bi, i, j: (layer, 0, n_f + j)),
            pl.BlockSpec((FFN_CONV_K, tf), lambda bi, i, j: (0, j)),
            pl.BlockSpec((FFN_CONV_K, tf), lambda bi, i, j: (0, n_f + j)),
            pl.BlockSpec((1, tf), lambda bi, i, j: (0, j)),
            pl.BlockSpec((1, tf), lambda bi, i, j: (0, n_f + j)),
            pl.BlockSpec((None, tf, d), lambda bi, i, j: (layer, j, 0)),
        ],
        out_specs=pl.BlockSpec((1, tm, d), lambda bi, i, j: (bi, i, 0)),
        out_shape=jax.ShapeDtypeStruct((b, s, d), F32),
        scratch_shapes=[pltpu.VMEM((tm + 2 * FFN_HALO, d), BF16), pltpu.VMEM((tm + 2 * FFN_HALO, d), BF16)],
        compiler_params=_params(3),
    )(x, x, x, gain, w_up, w_up, w_dw, w_dw, b_dw, b_dw, w_down)


def _rope_tables(seq_len):
    rows = seq_len // GRID_W
    row_pos = jnp.repeat(jnp.arange(rows, dtype=F32), GRID_W)
    col_pos = jnp.tile(jnp.arange(GRID_W, dtype=F32), rows)
    freqs = ROPE_THETA ** (-jnp.arange(AXIS_FREQS, dtype=F32) * 2.0 / AXIS_DIM)
    ang_r = row_pos[:, None] * freqs[None, :]
    ang_c = col_pos[:, None] * freqs[None, :]
    cr, sr, cc, sc = jnp.cos(ang_r), jnp.sin(ang_r), jnp.cos(ang_c), jnp.sin(ang_c)
    cos = jnp.concatenate([cr, cr, cc, cc], axis=-1)
    sin = jnp.concatenate([-sr, sr, -sc, sc], axis=-1)
    return cos, sin


def _qkv_kernel(x_ref, gain_ref, w_ref, qg_ref, kg_ref, cos_ref, sin_ref,
                q_ref, kt_ref, v_ref, *, scale):
    ts = x_ref.shape[1] // ROW_SUB_TILES
    qg = qg_ref[...]
    kg = kg_ref[...]
    lane = lax.broadcasted_iota(jnp.int32, (ts, HEAD_DIM), 1)
    first = (lane % (2 * AXIS_FREQS)) < AXIS_FREQS
    for c in range(ROW_SUB_TILES):
        rows = slice(c * ts, (c + 1) * ts)
        hn = _rms_norm(x_ref[0, rows, :], gain_ref[...]).astype(BF16)
        qkv = _dot(hn, w_ref[...])
        cos = cos_ref[rows, :]
        sin = sin_ref[rows, :]

        def head(col, gain):
            y = _rms_norm(qkv[:, col:col + HEAD_DIM], gain)
            partner = jnp.where(first,
                                pltpu.roll(y, HEAD_DIM - AXIS_FREQS, 1),
                                pltpu.roll(y, AXIS_FREQS, 1))
            return y * cos + partner * sin

        for h in range(N_HEADS):
            col = h * HEAD_DIM
            q_ref[0, rows, col:col + HEAD_DIM] = (head(col, qg) * scale).astype(BF16)
        for h in range(N_KV_HEADS):
            col = h * HEAD_DIM
            kt_ref[0, col:col + HEAD_DIM, rows] = head(ATTN_DIM + col, kg).T.astype(BF16)
        v_ref[0, rows, :] = qkv[:, ATTN_DIM + KV_DIM:].astype(BF16)


def _qkv(x, gain, w_qkv, q_gain, k_gain, cos, sin, layer):
    b, s, d = x.shape
    tm = _pick(s, 512)
    n = w_qkv.shape[2]
    return pl.pallas_call(
        functools.partial(_qkv_kernel, scale=HEAD_DIM ** -0.5 * LOG2_E),
        name="qkv",
        grid=(b, s // tm),
        in_specs=[
            pl.BlockSpec((1, tm, d), lambda bi, i: (bi, i, 0)),
            _resident((1, d)),
            _layer_resident((d, n), layer),
            _resident((1, HEAD_DIM)),
            _resident((1, HEAD_DIM)),
            pl.BlockSpec((tm, HEAD_DIM), lambda bi, i: (i, 0)),
            pl.BlockSpec((tm, HEAD_DIM), lambda bi, i: (i, 0)),
        ],
        out_specs=[
            pl.BlockSpec((1, tm, ATTN_DIM), lambda bi, i: (bi, i, 0)),
            pl.BlockSpec((1, KV_DIM, tm), lambda bi, i: (bi, 0, i)),
            pl.BlockSpec((1, tm, KV_DIM), lambda bi, i: (bi, i, 0)),
        ],
        out_shape=[
            jax.ShapeDtypeStruct((b, s, ATTN_DIM), BF16),
            jax.ShapeDtypeStruct((b, KV_DIM, s), BF16),
            jax.ShapeDtypeStruct((b, s, KV_DIM), BF16),
        ],
        compiler_params=_params(2),
    )(x, gain.reshape(1, d), w_qkv, q_gain.reshape(1, HEAD_DIM), k_gain.reshape(1, HEAD_DIM), cos, sin)


def _attn_kernel(q_ref, kt_ref, v_ref, o_ref, *, tq, tk, n_chunks):
    q = jnp.concatenate(
        [q_ref[0, :, g * HEAD_DIM:(g + 1) * HEAD_DIM] for g in range(GQA_GROUP)], axis=0)
    m = l = acc = None
    for c in range(n_chunks):
        s = _dot(q, kt_ref[0, :, c * tk:(c + 1) * tk])
        m_c = jnp.max(s, axis=-1, keepdims=True)
        if c == 0:
            m_new = m_c
            p = jnp.exp2(s - m_new)
            l = jnp.sum(p, axis=-1, keepdims=True)
            acc = _dot(p.astype(BF16), v_ref[0, c * tk:(c + 1) * tk, :])
        else:
            m_new = jnp.maximum(m, m_c)
            alpha = jnp.exp2(m - m_new)
            p = jnp.exp2(s - m_new)
            l = alpha * l + jnp.sum(p, axis=-1, keepdims=True)
            acc = alpha * acc + _dot(p.astype(BF16), v_ref[0, c * tk:(c + 1) * tk, :])
        m = m_new
    out = acc / l
    for g in range(GQA_GROUP):
        o_ref[0, :, g * HEAD_DIM:(g + 1) * HEAD_DIM] = out[g * tq:(g + 1) * tq].astype(BF16)


def _attention(q, kt, v):
    b, s, _ = q.shape
    tq = _pick(s, ATTN_TQ)
    tk = _pick(s, min(ATTN_TK, max(s // ATTN_MIN_CHUNKS, V7X_LANES)))
    gw = GQA_GROUP * HEAD_DIM
    return pl.pallas_call(
        functools.partial(_attn_kernel, tq=tq, tk=tk, n_chunks=s // tk),
        name="attn",
        grid=(b, N_KV_HEADS, s // tq),
        in_specs=[
            pl.BlockSpec((1, tq, gw), lambda bi, h, i: (bi, i, h)),
            pl.BlockSpec((1, HEAD_DIM, s), lambda bi, h, i: (bi, h, 0)),
            pl.BlockSpec((1, s, HEAD_DIM), lambda bi, h, i: (bi, 0, h)),
        ],
        out_specs=pl.BlockSpec((1, tq, gw), lambda bi, h, i: (bi, i, h)),
        out_shape=jax.ShapeDtypeStruct((b, s, ATTN_DIM), BF16),
        compiler_params=_params(3),
    )(q, kt, v)


def _oproj_kernel(a_ref, x_ref, w_ref, o_ref):
    o_ref[0] = x_ref[0] + _dot(a_ref[0], w_ref[...])


def _oproj(a, x, w_o, layer):
    b, s, d = x.shape
    tm = _pick(s, 512)
    return pl.pallas_call(
        _oproj_kernel,
        name="oproj",
        grid=(b, s // tm),
        in_specs=[
            pl.BlockSpec((1, tm, ATTN_DIM), lambda bi, i: (bi, i, 0)),
            pl.BlockSpec((1, tm, d), lambda bi, i: (bi, i, 0)),
            _layer_resident((ATTN_DIM, d), layer),
        ],
        out_specs=pl.BlockSpec((1, tm, d), lambda bi, i: (bi, i, 0)),
        out_shape=jax.ShapeDtypeStruct((b, s, d), F32),
        compiler_params=_params(2),
    )(a, x, w_o)


def _pw1_kernel(x_ref, gain_ref, w_ref, b_ref, u_ref, *, d):
    ts = x_ref.shape[1] // ROW_SUB_TILES
    for c in range(ROW_SUB_TILES):
        rows = slice(c * ts, (c + 1) * ts)
        hn = _rms_norm(x_ref[0, rows, :], gain_ref[...]).astype(BF16)
        u = _dot(hn, w_ref[...]) + b_ref[...]
        u_ref[0, rows, :] = u[:, :d] * _sigmoid(u[:, d:])


def _pw1(x, gain, w_pw1, b_pw1, layer):
    b, s, d = x.shape
    tm = _pick(s, 512)
    return pl.pallas_call(
        functools.partial(_pw1_kernel, d=d),
        name="pw1",
        grid=(b, s // tm),
        in_specs=[
            pl.BlockSpec((1, tm, d), lambda bi, i: (bi, i, 0)),
            _resident((1, d)),
            _layer_resident((d, 2 * d), layer),
            _resident((1, 2 * d)),
        ],
        out_specs=pl.BlockSpec((1, tm, d), lambda bi, i: (bi, i, 0)),
        out_shape=jax.ShapeDtypeStruct((b, s, d), F32),
        compiler_params=_params(2),
    )(x, gain.reshape(1, d), w_pw1, b_pw1.reshape(1, 2 * d))


def _convpw2_kernel(up_ref, u_ref, un_ref, x_ref, wdw_ref, bdw_ref, lng_ref, lnb_ref,
                    w2_ref, b2_ref, o_ref, ext_ref, conv_ref, *, tm, cw, n_seq_tiles):
    i = pl.program_id(1)
    rows = tm + 2 * CONV_HALO
    d = conv_ref.shape[1]
    ext_ref[0:CONV_HALO, :] = jnp.where(i > 0, up_ref[0], 0.0)
    ext_ref[CONV_HALO:CONV_HALO + tm, :] = u_ref[0]
    ext_ref[CONV_HALO + tm:rows, :] = jnp.where(i < n_seq_tiles - 1, un_ref[0], 0.0)

    base = CONV_HALO - CONV_K // 2
    for c in range(d // cw):
        cols = slice(c * cw, (c + 1) * cw)
        ue = ext_ref[:, cols]
        acc = None
        for r in range(V7X_SUBLANES):
            ur = ue if r == 0 else pltpu.roll(ue, rows - r, 0)
            for a in range(2 * CONV_HALO // V7X_SUBLANES):
                k = a * V7X_SUBLANES + r - base
                if 0 <= k < CONV_K:
                    term = ur[a * V7X_SUBLANES:a * V7X_SUBLANES + tm] * wdw_ref[k:k + 1, cols]
                    acc = term if acc is None else acc + term
        conv_ref[:, cols] = acc + bdw_ref[:, cols]

    y = conv_ref[...]
    mu = jnp.mean(y, axis=-1, keepdims=True)
    yc = y - mu
    var = jnp.mean(yc * yc, axis=-1, keepdims=True)
    y = yc * lax.rsqrt(var + EPS) * lng_ref[...] + lnb_ref[...]
    y = y * _sigmoid(y)
    o_ref[0] = x_ref[0] + (_dot(y.astype(BF16), w2_ref[...]) + b2_ref[...])


def _convpw2(u, x, w_dw, b_dw, ln_g, ln_b, w_pw2, b_pw2, layer):
    b, s, d = x.shape
    tm = _pick(s, 512)
    cw = _pick(d, 512)
    n_seq = s // tm
    hb = tm // CONV_HALO
    last_hb = s // CONV_HALO - 1
    row = lambda a: a.reshape(1, d)
    return pl.pallas_call(
        functools.partial(_convpw2_kernel, tm=tm, cw=cw, n_seq_tiles=n_seq),
        name="convpw2",
        grid=(b, n_seq),
        in_specs=[
            pl.BlockSpec((1, CONV_HALO, d), lambda bi, i: (bi, jnp.maximum(i * hb - 1, 0), 0)),
            pl.BlockSpec((1, tm, d), lambda bi, i: (bi, i, 0)),
            pl.BlockSpec((1, CONV_HALO, d), lambda bi, i: (bi, jnp.minimum((i + 1) * hb, last_hb), 0)),
            pl.BlockSpec((1, tm, d), lambda bi, i: (bi, i, 0)),
            _resident((CONV_K, d)),
            _resident((1, d)),
            _resident((1, d)),
            _resident((1, d)),
            _layer_resident((d, d), layer),
            _resident((1, d)),
        ],
        out_specs=pl.BlockSpec((1, tm, d), lambda bi, i: (bi, i, 0)),
        out_shape=jax.ShapeDtypeStruct((b, s, d), F32),
        scratch_shapes=[pltpu.VMEM((tm + 2 * CONV_HALO, d), F32), pltpu.VMEM((tm, d), F32)],
        compiler_params=_params(2),
    )(u, u, u, x, w_dw, row(b_dw), row(ln_g), row(ln_b), w_pw2, row(b_pw2))


def _trunk(x, p):
    depth = p["ffn_norm"].shape[0]
    cos, sin = _rope_tables(x.shape[1])
    for i in range(depth):
        j = i // 2
        if i % 2 == 0:
            q, kt, v = _qkv(x, p["attn_norm"][j], p["attn_w_qkv"], p["attn_q_norm"][j],
                            p["attn_k_norm"][j], cos, sin, j)
            x = _oproj(_attention(q, kt, v), x, p["attn_w_o"], j)
        else:
            u = _pw1(x, p["conv_norm"][j], p["conv_w_pw1"], p["conv_b_pw1"][j], j)
            x = _convpw2(u, x, p["conv_w_dw"][j], p["conv_b_dw"][j], p["conv_ln_g"][j],
                         p["conv_ln_b"][j], p["conv_w_pw2"], p["conv_b_pw2"][j], j)
        x = _ffn(x, p["ffn_norm"][i], p["ffn_w_up"], p["ffn_w_dw"][i], p["ffn_b_dw"][i],
                 p["ffn_w_down"], i)
    return x


def kernel(x_prompt, x_sample, attn_norm, attn_w_qkv, attn_q_norm, attn_k_norm, attn_w_o,
           conv_norm, conv_w_pw1, conv_b_pw1, conv_w_dw, conv_b_dw, conv_ln_g, conv_ln_b,
           conv_w_pw2, conv_b_pw2, ffn_norm, ffn_w_up, ffn_w_dw, ffn_b_dw, ffn_w_down):
    p = dict(
        attn_norm=attn_norm, attn_w_qkv=attn_w_qkv.astype(BF16), attn_q_norm=attn_q_norm,
        attn_k_norm=attn_k_norm, attn_w_o=attn_w_o.astype(BF16),
        conv_norm=conv_norm, conv_w_pw1=conv_w_pw1.astype(BF16), conv_b_pw1=conv_b_pw1,
        conv_w_dw=conv_w_dw, conv_b_dw=conv_b_dw, conv_ln_g=conv_ln_g, conv_ln_b=conv_ln_b,
        conv_w_pw2=conv_w_pw2.astype(BF16), conv_b_pw2=conv_b_pw2,
        ffn_norm=ffn_norm, ffn_w_up=ffn_w_up.astype(BF16), ffn_w_dw=ffn_w_dw, ffn_b_dw=ffn_b_dw,
        ffn_w_down=ffn_w_down.astype(BF16))
    return _trunk(x_prompt, p), _trunk(x_sample, p)
```

```python
import functools

import jax
import jax.numpy as jnp
from jax import lax
from jax.experimental import pallas as pl
from jax.experimental.pallas import tpu as pltpu

N_HEADS = 16
N_KV_HEADS = 4
HEAD_DIM = 128
GQA_GROUP = N_HEADS // N_KV_HEADS
ATTN_DIM = N_HEADS * HEAD_DIM
KV_DIM = N_KV_HEADS * HEAD_DIM
ROPE_THETA = 10000.0
AXIS_DIM = HEAD_DIM // 2
AXIS_FREQS = AXIS_DIM // 2
GRID_W = 64
CONV_K = 31
FFN_CONV_K = 3
EPS = 1e-6

V7X_SUBLANES = 8
V7X_LANES = 128
V7X_VMEM_LIMIT_BYTES = 56 * 1024 * 1024

FFN_HALO = V7X_SUBLANES
CONV_HALO = 2 * V7X_SUBLANES

ATTN_TQ = 256
ATTN_TK = 2048
ATTN_MIN_CHUNKS = 4
ROW_SUB_TILES = 2
LOG2_E = 1.4426950408889634

BF16 = jnp.bfloat16
F32 = jnp.float32


def _pick(total, preferred):
    if total <= preferred:
        return total
    t = preferred
    while t >= V7X_LANES:
        if total % t == 0:
            return t
        t -= V7X_LANES
    return total


def _params(n_grid):
    return pltpu.CompilerParams(
        dimension_semantics=("arbitrary",) * n_grid,
        vmem_limit_bytes=V7X_VMEM_LIMIT_BYTES)


def _resident(shape):
    return pl.BlockSpec(shape, lambda *_: (0,) * len(shape), pipeline_mode=pl.Buffered(1))


def _layer_resident(shape, layer):
    return pl.BlockSpec((None,) + shape, lambda *_: (layer,) + (0,) * len(shape),
                        pipeline_mode=pl.Buffered(1))


def _rms_norm(x, gain):
    y = x * lax.rsqrt(jnp.mean(x * x, axis=-1, keepdims=True) + EPS)
    return y * gain


def _sigmoid(x):
    return 0.5 * jnp.tanh(0.5 * x) + 0.5


def _dot(a, b):
    return jnp.dot(a, b, preferred_element_type=F32)


def _ffn_kernel(xp_ref, x_ref, xn_ref, gain_ref, wg_ref, wv_ref, dwg_ref, dwv_ref,
                bg_ref, bv_ref, wd_ref, o_ref, hn_ref, *, tm, n_seq_tiles):
    i = pl.program_id(1)
    j = pl.program_id(2)
    rows = tm + 2 * FFN_HALO

    @pl.when(j == 0)
    def _():
        gain = gain_ref[...]
        x = x_ref[0]
        hn_ref[FFN_HALO:FFN_HALO + tm, :] = _rms_norm(x, gain).astype(BF16)
        prev = jnp.where(i > 0, _rms_norm(xp_ref[0], gain), 0.0)
        hn_ref[0:FFN_HALO, :] = prev.astype(BF16)
        nxt = jnp.where(i < n_seq_tiles - 1, _rms_norm(xn_ref[0], gain), 0.0)
        hn_ref[FFN_HALO + tm:rows, :] = nxt.astype(BF16)
        o_ref[0] = x

    hn = hn_ref[...]

    def conv(w_ref, dw_ref, b_ref):
        u = _dot(hn, w_ref[...])
        dw = dw_ref[...]
        lo = pltpu.roll(u, 1, 0)
        hi = pltpu.roll(u, rows - 1, 0)
        s = slice(FFN_HALO, FFN_HALO + tm)
        return lo[s] * dw[0:1] + u[s] * dw[1:2] + hi[s] * dw[2:3] + b_ref[...]

    g = conv(wg_ref, dwg_ref, bg_ref)
    v = conv(wv_ref, dwv_ref, bv_ref)
    act = (g * _sigmoid(g)) * v
    o_ref[0] += _dot(act.astype(BF16), wd_ref[...])


def _ffn(x, gain, w_up, w_dw, b_dw, w_down, layer):
    b, s, d = x.shape
    f = w_down.shape[1]
    tm = _pick(s, 512)
    tf = _pick(f, 512)
    n_seq, n_f = s // tm, f // tf
    hb = tm // FFN_HALO
    last_hb = s // FFN_HALO - 1
    gain = gain.reshape(1, d)
    b_dw = b_dw.reshape(1, 2 * f)
    n_tiles = b * n_seq
    x_tiles = x.reshape(n_tiles, tm, d)
    shift_at = max(n_f // 2, 1)

    def x_tile(bi, i, j):
        nxt = jnp.where(j >= shift_at, 1, 0)
        return (jnp.minimum(bi * n_seq + i + nxt, n_tiles - 1), 0, 0)

    return pl.pallas_call(
        functools.partial(_ffn_kernel, tm=tm, n_seq_tiles=n_seq),
        name="ffn",
        grid=(b, n_seq, n_f),
        in_specs=[
            pl.BlockSpec((1, FFN_HALO, d), lambda bi, i, j: (bi, jnp.maximum(i * hb - 1, 0), 0)),
            pl.BlockSpec((1, tm, d), x_tile),
            pl.BlockSpec((1, FFN_HALO, d), lambda bi, i, j: (bi, jnp.minimum((i + 1) * hb, last_hb), 0)),
            pl.BlockSpec((1, d), lambda bi, i, j: (0, 0)),
            pl.BlockSpec((None, d, tf), lambda bi, i, j: (layer, 0, j)),
            pl.BlockSpec((None, d, tf), lambda bi, i, j: (layer, 0, n_f + j)),
            pl.BlockSpec((FFN_CONV_K, tf), lambda bi, i, j: (0, j)),
            pl.BlockSpec((FFN_CONV_K, tf), lambda bi, i, j: (0, n_f + j)),
            pl.BlockSpec((1, tf), lambda bi, i, j: (0, j)),
            pl.BlockSpec((1, tf), lambda bi, i, j: (0, n_f + j)),
            pl.BlockSpec((None, tf, d), lambda bi, i, j: (layer, j, 0)),
        ],
        out_specs=pl.BlockSpec((1, tm, d), lambda bi, i, j: (bi, i, 0)),
        out_shape=jax.ShapeDtypeStruct((b, s, d), F32),
        scratch_shapes=[pltpu.VMEM((tm + 2 * FFN_HALO, d), BF16)],
        compiler_params=_params(3),
    )(x, x_tiles, x, gain, w_up, w_up, w_dw, w_dw, b_dw, b_dw, w_down)


def _rope_tables(seq_len):
    rows = seq_len // GRID_W
    row_pos = jnp.repeat(jnp.arange(rows, dtype=F32), GRID_W)
    col_pos = jnp.tile(jnp.arange(GRID_W, dtype=F32), rows)
    freqs = ROPE_THETA ** (-jnp.arange(AXIS_FREQS, dtype=F32) * 2.0 / AXIS_DIM)
    ang_r = row_pos[:, None] * freqs[None, :]
    ang_c = col_pos[:, None] * freqs[None, :]
    cr, sr, cc, sc = jnp.cos(ang_r), jnp.sin(ang_r), jnp.cos(ang_c), jnp.sin(ang_c)
    cos = jnp.concatenate([cr, cr, cc, cc], axis=-1)
    sin = jnp.concatenate([-sr, sr, -sc, sc], axis=-1)
    return cos, sin


def _qkv_kernel(x_ref, gain_ref, w_ref, qg_ref, kg_ref, cos_ref, sin_ref,
                q_ref, kt_ref, v_ref, *, scale):
    ts = x_ref.shape[1] // ROW_SUB_TILES
    qg = qg_ref[...]
    kg = kg_ref[...]
    lane = lax.broadcasted_iota(jnp.int32, (ts, HEAD_DIM), 1)
    first = (lane % (2 * AXIS_FREQS)) < AXIS_FREQS
    for c in range(ROW_SUB_TILES):
        rows = slice(c * ts, (c + 1) * ts)
        hn = _rms_norm(x_ref[0, rows, :], gain_ref[...]).astype(BF16)
        qkv = _dot(hn, w_ref[...])
        cos = cos_ref[rows, :]
        sin = sin_ref[rows, :]

        def head(col, gain):
            y = _rms_norm(qkv[:, col:col + HEAD_DIM], gain)
            partner = jnp.where(first,
                                pltpu.roll(y, HEAD_DIM - AXIS_FREQS, 1),
                                pltpu.roll(y, AXIS_FREQS, 1))
            return y * cos + partner * sin

        for h in range(N_HEADS):
            col = h * HEAD_DIM
            q_ref[0, rows, col:col + HEAD_DIM] = (head(col, qg) * scale).astype(BF16)
        for h in range(N_KV_HEADS):
            col = h * HEAD_DIM
            kt_ref[0, col:col + HEAD_DIM, rows] = head(ATTN_DIM + col, kg).T.astype(BF16)
        v_ref[0, rows, :] = qkv[:, ATTN_DIM + KV_DIM:].astype(BF16)


def _qkv(x, gain, w_qkv, q_gain, k_gain, cos, sin, layer):
    b, s, d = x.shape
    tm = _pick(s, 512)
    n = w_qkv.shape[2]
    return pl.pallas_call(
        functools.partial(_qkv_kernel, scale=HEAD_DIM ** -0.5 * LOG2_E),
        name="qkv",
        grid=(b, s // tm),
        in_specs=[
            pl.BlockSpec((1, tm, d), lambda bi, i: (bi, i, 0)),
            _resident((1, d)),
            _layer_resident((d, n), layer),
            _resident((1, HEAD_DIM)),
            _resident((1, HEAD_DIM)),
            pl.BlockSpec((tm, HEAD_DIM), lambda bi, i: (i, 0)),
            pl.BlockSpec((tm, HEAD_DIM), lambda bi, i: (i, 0)),
        ],
        out_specs=[
            pl.BlockSpec((1, tm, ATTN_DIM), lambda bi, i: (bi, i, 0)),
            pl.BlockSpec((1, KV_DIM, tm), lambda bi, i: (bi, 0, i)),
            pl.BlockSpec((1, tm, KV_DIM), lambda bi, i: (bi, i, 0)),
        ],
        out_shape=[
            jax.ShapeDtypeStruct((b, s, ATTN_DIM), BF16),
            jax.ShapeDtypeStruct((b, KV_DIM, s), BF16),
            jax.ShapeDtypeStruct((b, s, KV_DIM), BF16),
        ],
        compiler_params=_params(2),
    )(x, gain.reshape(1, d), w_qkv, q_gain.reshape(1, HEAD_DIM), k_gain.reshape(1, HEAD_DIM), cos, sin)


def _attn_kernel(q_ref, kt_ref, v_ref, o_ref, *, tq, tk, n_chunks):
    q = jnp.concatenate(
        [q_ref[0, :, g * HEAD_DIM:(g + 1) * HEAD_DIM] for g in range(GQA_GROUP)], axis=0)
    m = l = acc = None
    for c in range(n_chunks):
        s = _dot(q, kt_ref[0, :, c * tk:(c + 1) * tk])
        m_c = jnp.max(s, axis=-1, keepdims=True)
        if c == 0:
            m_new = m_c
            p = jnp.exp2(s - m_new)
            l = jnp.sum(p, axis=-1, keepdims=True)
            acc = _dot(p.astype(BF16), v_ref[0, c * tk:(c + 1) * tk, :])
        else:
            m_new = jnp.maximum(m, m_c)
            alpha = jnp.exp2(m - m_new)
            p = jnp.exp2(s - m_new)
            l = alpha * l + jnp.sum(p, axis=-1, keepdims=True)
            acc = alpha * acc + _dot(p.astype(BF16), v_ref[0, c * tk:(c + 1) * tk, :])
        m = m_new
    out = acc / l
    for g in range(GQA_GROUP):
        o_ref[0, :, g * HEAD_DIM:(g + 1) * HEAD_DIM] = out[g * tq:(g + 1) * tq].astype(BF16)


def _attention(q, kt, v):
    b, s, _ = q.shape
    tq = _pick(s, ATTN_TQ)
    tk = _pick(s, min(ATTN_TK, max(s // ATTN_MIN_CHUNKS, V7X_LANES)))
    gw = GQA_GROUP * HEAD_DIM
    return pl.pallas_call(
        functools.partial(_attn_kernel, tq=tq, tk=tk, n_chunks=s // tk),
        name="attn",
        grid=(b, N_KV_HEADS, s // tq),
        in_specs=[
            pl.BlockSpec((1, tq, gw), lambda bi, h, i: (bi, i, h)),
            pl.BlockSpec((1, HEAD_DIM, s), lambda bi, h, i: (bi, h, 0)),
            pl.BlockSpec((1, s, HEAD_DIM), lambda bi, h, i: (bi, 0, h)),
        ],
        out_specs=pl.BlockSpec((1, tq, gw), lambda bi, h, i: (bi, i, h)),
        out_shape=jax.ShapeDtypeStruct((b, s, ATTN_DIM), BF16),
        compiler_params=_params(3),
    )(q, kt, v)


def _oproj_kernel(a_ref, x_ref, w_ref, o_ref):
    o_ref[0] = x_ref[0] + _dot(a_ref[0], w_ref[...])


def _oproj(a, x, w_o, layer):
    b, s, d = x.shape
    tm = _pick(s, 512)
    return pl.pallas_call(
        _oproj_kernel,
        name="oproj",
        grid=(b, s // tm),
        in_specs=[
            pl.BlockSpec((1, tm, ATTN_DIM), lambda bi, i: (bi, i, 0)),
            pl.BlockSpec((1, tm, d), lambda bi, i: (bi, i, 0)),
            _layer_resident((ATTN_DIM, d), layer),
        ],
        out_specs=pl.BlockSpec((1, tm, d), lambda bi, i: (bi, i, 0)),
        out_shape=jax.ShapeDtypeStruct((b, s, d), F32),
        compiler_params=_params(2),
    )(a, x, w_o)


def _pw1_kernel(x_ref, gain_ref, w_ref, b_ref, u_ref, *, d):
    ts = x_ref.shape[1] // ROW_SUB_TILES
    for c in range(ROW_SUB_TILES):
        rows = slice(c * ts, (c + 1) * ts)
        hn = _rms_norm(x_ref[0, rows, :], gain_ref[...]).astype(BF16)
        u = _dot(hn, w_ref[...]) + b_ref[...]
        u_ref[0, rows, :] = u[:, :d] * _sigmoid(u[:, d:])


def _pw1(x, gain, w_pw1, b_pw1, layer):
    b, s, d = x.shape
    tm = _pick(s, 512)
    return pl.pallas_call(
        functools.partial(_pw1_kernel, d=d),
        name="pw1",
        grid=(b, s // tm),
        in_specs=[
            pl.BlockSpec((1, tm, d), lambda bi, i: (bi, i, 0)),
            _resident((1, d)),
            _layer_resident((d, 2 * d), layer),
            _resident((1, 2 * d)),
        ],
        out_specs=pl.BlockSpec((1, tm, d), lambda bi, i: (bi, i, 0)),
        out_shape=jax.ShapeDtypeStruct((b, s, d), F32),
        compiler_params=_params(2),
    )(x, gain.reshape(1, d), w_pw1, b_pw1.reshape(1, 2 * d))


def _convpw2_kernel(up_ref, u_ref, un_ref, x_ref, wdw_ref, bdw_ref, lng_ref, lnb_ref,
                    w2_ref, b2_ref, o_ref, ext_ref, conv_ref, *, tm, cw, n_seq_tiles):
    i = pl.program_id(1)
    rows = tm + 2 * CONV_HALO
    d = conv_ref.shape[1]
    ext_ref[0:CONV_HALO, :] = jnp.where(i > 0, up_ref[0], 0.0)
    ext_ref[CONV_HALO:CONV_HALO + tm, :] = u_ref[0]
    ext_ref[CONV_HALO + tm:rows, :] = jnp.where(i < n_seq_tiles - 1, un_ref[0], 0.0)

    base = CONV_HALO - CONV_K // 2
    for c in range(d // cw):
        cols = slice(c * cw, (c + 1) * cw)
        ue = ext_ref[:, cols]
        acc = None
        for r in range(V7X_SUBLANES):
            ur = ue if r == 0 else pltpu.roll(ue, rows - r, 0)
            for a in range(2 * CONV_HALO // V7X_SUBLANES):
                k = a * V7X_SUBLANES + r - base
                if 0 <= k < CONV_K:
                    term = ur[a * V7X_SUBLANES:a * V7X_SUBLANES + tm] * wdw_ref[k:k + 1, cols]
                    acc = term if acc is None else acc + term
        conv_ref[:, cols] = acc + bdw_ref[:, cols]

    y = conv_ref[...]
    mu = jnp.mean(y, axis=-1, keepdims=True)
    yc = y - mu
    var = jnp.mean(yc * yc, axis=-1, keepdims=True)
    y = yc * lax.rsqrt(var + EPS) * lng_ref[...] + lnb_ref[...]
    y = y * _sigmoid(y)
    o_ref[0] = x_ref[0] + (_dot(y.astype(BF16), w2_ref[...]) + b2_ref[...])


def _convpw2(u, x, w_dw, b_dw, ln_g, ln_b, w_pw2, b_pw2, layer):
    b, s, d = x.shape
    tm = _pick(s, 512)
    cw = _pick(d, 512)
    n_seq = s // tm
    hb = tm // CONV_HALO
    last_hb = s // CONV_HALO - 1
    row = lambda a: a.reshape(1, d)
    return pl.pallas_call(
        functools.partial(_convpw2_kernel, tm=tm, cw=cw, n_seq_tiles=n_seq),
        name="convpw2",
        grid=(b, n_seq),
        in_specs=[
            pl.BlockSpec((1, CONV_HALO, d), lambda bi, i: (bi, jnp.maximum(i * hb - 1, 0), 0)),
            pl.BlockSpec((1, tm, d), lambda bi, i: (bi, i, 0)),
            pl.BlockSpec((1, CONV_HALO, d), lambda bi, i: (bi, jnp.minimum((i + 1) * hb, last_hb), 0)),
            pl.BlockSpec((1, tm, d), lambda bi, i: (bi, i, 0)),
            _resident((CONV_K, d)),
            _resident((1, d)),
            _resident((1, d)),
            _resident((1, d)),
            _layer_resident((d, d), layer),
            _resident((1, d)),
        ],
        out_specs=pl.BlockSpec((1, tm, d), lambda bi, i: (bi, i, 0)),
        out_shape=jax.ShapeDtypeStruct((b, s, d), F32),
        scratch_shapes=[pltpu.VMEM((tm + 2 * CONV_HALO, d), F32), pltpu.VMEM((tm, d), F32)],
        compiler_params=_params(2),
    )(u, u, u, x, w_dw, row(b_dw), row(ln_g), row(ln_b), w_pw2, row(b_pw2))


def _trunk(x, p):
    depth = p["ffn_norm"].shape[0]
    cos, sin = _rope_tables(x.shape[1])
    for i in range(depth):
        j = i // 2
        if i % 2 == 0:
            q, kt, v = _qkv(x, p["attn_norm"][j], p["attn_w_qkv"], p["attn_q_norm"][j],
                            p["attn_k_norm"][j], cos, sin, j)
            x = _oproj(_attention(q, kt, v), x, p["attn_w_o"], j)
        else:
            u = _pw1(x, p["conv_norm"][j], p["conv_w_pw1"], p["conv_b_pw1"][j], j)
            x = _convpw2(u, x, p["conv_w_dw"][j], p["conv_b_dw"][j], p["conv_ln_g"][j],
                         p["conv_ln_b"][j], p["conv_w_pw2"], p["conv_b_pw2"][j], j)
        x = _ffn(x, p["ffn_norm"][i], p["ffn_w_up"], p["ffn_w_dw"][i], p["ffn_b_dw"][i],
                 p["ffn_w_down"], i)
    return x


def kernel(x_prompt, x_sample, attn_norm, attn_w_qkv, attn_q_norm, attn_k_norm, attn_w_o,
           conv_norm, conv_w_pw1, conv_b_pw1, conv_w_dw, conv_b_dw, conv_ln_g, conv_ln_b,
           conv_w_pw2, conv_b_pw2, ffn_norm, ffn_w_up, ffn_w_dw, ffn_b_dw, ffn_w_down):
    p = dict(
        attn_norm=attn_norm, attn_w_qkv=attn_w_qkv.astype(BF16), attn_q_norm=attn_q_norm,
        attn_k_norm=attn_k_norm, attn_w_o=attn_w_o.astype(BF16),
        conv_norm=conv_norm, conv_w_pw1=conv_w_pw1.astype(BF16), conv_b_pw1=conv_b_pw1,
        conv_w_dw=conv_w_dw, conv_b_dw=conv_b_dw, conv_ln_g=conv_ln_g, conv_ln_b=conv_ln_b,
        conv_w_pw2=conv_w_pw2.astype(BF16), conv_b_pw2=conv_b_pw2,
        ffn_norm=ffn_norm, ffn_w_up=ffn_w_up.astype(BF16), ffn_w_dw=ffn_w_dw, ffn_b_dw=ffn_b_dw,
        ffn_w_down=ffn_w_down.astype(BF16))
    return _trunk(x_prompt, p), _trunk(x_sample, p)
```

```python
import functools

import jax
import jax.numpy as jnp
from jax import lax
from jax.experimental import pallas as pl
from jax.experimental.pallas import tpu as pltpu

N_HEADS = 16
N_KV_HEADS = 4
HEAD_DIM = 128
GQA_GROUP = N_HEADS // N_KV_HEADS
ATTN_DIM = N_HEADS * HEAD_DIM
KV_DIM = N_KV_HEADS * HEAD_DIM
ROPE_THETA = 10000.0
AXIS_DIM = HEAD_DIM // 2
AXIS_FREQS = AXIS_DIM // 2
GRID_W = 64
CONV_K = 31
FFN_CONV_K = 3
EPS = 1e-6

V7X_SUBLANES = 8
V7X_LANES = 128
V7X_VMEM_LIMIT_BYTES = 56 * 1024 * 1024

FFN_HALO = V7X_SUBLANES
CONV_HALO = 2 * V7X_SUBLANES

ATTN_TQ = 256
ATTN_TK = 2048
ATTN_MIN_CHUNKS = 4
ROW_SUB_TILES = 2
LOG2_E = 1.4426950408889634

BF16 = jnp.bfloat16
F32 = jnp.float32


def _pick(total, preferred):
    if total <= preferred:
        return total
    t = preferred
    while t >= V7X_LANES:
        if total % t == 0:
            return t
        t -= V7X_LANES
    return total


def _params(n_grid):
    return pltpu.CompilerParams(
        dimension_semantics=("arbitrary",) * n_grid,
        vmem_limit_bytes=V7X_VMEM_LIMIT_BYTES)


def _resident(shape):
    return pl.BlockSpec(shape, lambda *_: (0,) * len(shape), pipeline_mode=pl.Buffered(1))


def _layer_resident(shape, layer):
    return pl.BlockSpec((None,) + shape, lambda *_: (layer,) + (0,) * len(shape),
                        pipeline_mode=pl.Buffered(1))


def _rms_norm(x, gain):
    y = x * lax.rsqrt(jnp.mean(x * x, axis=-1, keepdims=True) + EPS)
    return y * gain


def _sigmoid(x):
    return 0.5 * jnp.tanh(0.5 * x) + 0.5


def _dot(a, b):
    return jnp.dot(a, b, preferred_element_type=F32)


def _ffn_kernel(xp_ref, x_ref, xn_ref, gain_ref, wg_ref, wv_ref, dwg_ref, dwv_ref,
                bg_ref, bv_ref, wd_ref, o_ref, hn_ref, *, tm, n_seq_tiles):
    i = pl.program_id(1)
    j = pl.program_id(2)
    rows = tm + 2 * FFN_HALO

    @pl.when(j == 0)
    def _():
        gain = gain_ref[...]
        x = x_ref[0]
        hn_ref[FFN_HALO:FFN_HALO + tm, :] = _rms_norm(x, gain).astype(BF16)
        prev = jnp.where(i > 0, _rms_norm(xp_ref[0], gain), 0.0)
        hn_ref[0:FFN_HALO, :] = prev.astype(BF16)
        nxt = jnp.where(i < n_seq_tiles - 1, _rms_norm(xn_ref[0], gain), 0.0)
        hn_ref[FFN_HALO + tm:rows, :] = nxt.astype(BF16)
        o_ref[0] = x

    hn = hn_ref[...]

    def conv(w_ref, dw_ref, b_ref):
        u = _dot(hn, w_ref[...])
        dw = dw_ref[...]
        lo = pltpu.roll(u, 1, 0)
        hi = pltpu.roll(u, rows - 1, 0)
        s = slice(FFN_HALO, FFN_HALO + tm)
        return lo[s] * dw[0:1] + u[s] * dw[1:2] + hi[s] * dw[2:3] + b_ref[...]

    g = conv(wg_ref, dwg_ref, bg_ref)
    v = conv(wv_ref, dwv_ref, bv_ref)
    act = (g * _sigmoid(g)) * v
    o_ref[0] += _dot(act.astype(BF16), wd_ref[...])


def _ffn_f_tile(f):
    return _pick(f, 512)


def _tile_up_weights(w_up):
    n_layers, d, f2 = w_up.shape
    tf = _ffn_f_tile(f2 // 2)
    return w_up.reshape(n_layers, d, f2 // tf, tf).transpose(0, 2, 1, 3)


def _ffn(x, gain, w_up, w_dw, b_dw, w_down, layer):
    b, s, d = x.shape
    f = w_down.shape[1]
    tm = _pick(s, 512)
    tf = _ffn_f_tile(f)
    n_seq, n_f = s // tm, f // tf
    hb = tm // FFN_HALO
    last_hb = s // FFN_HALO - 1
    gain = gain.reshape(1, d)
    b_dw = b_dw.reshape(1, 2 * f)

    return pl.pallas_call(
        functools.partial(_ffn_kernel, tm=tm, n_seq_tiles=n_seq),
        name="ffn",
        grid=(b, n_seq, n_f),
        in_specs=[
            pl.BlockSpec((1, FFN_HALO, d), lambda bi, i, j: (bi, jnp.maximum(i * hb - 1, 0), 0)),
            pl.BlockSpec((1, tm, d), lambda bi, i, j: (bi, i, 0)),
            pl.BlockSpec((1, FFN_HALO, d), lambda bi, i, j: (bi, jnp.minimum((i + 1) * hb, last_hb), 0)),
            pl.BlockSpec((1, d), lambda bi, i, j: (0, 0)),
            pl.BlockSpec((None, None, d, tf), lambda bi, i, j: (layer, j, 0, 0)),
            pl.BlockSpec((None, None, d, tf), lambda bi, i, j: (layer, n_f + j, 0, 0)),
            pl.BlockSpec((FFN_CONV_K, tf), lambda bi, i, j: (0, j)),
            pl.BlockSpec((FFN_CONV_K, tf), lambda bi, i, j: (0, n_f + j)),
            pl.BlockSpec((1, tf), lambda bi, i, j: (0, j)),
            pl.BlockSpec((1, tf), lambda bi, i, j: (0, n_f + j)),
            pl.BlockSpec((None, tf, d), lambda bi, i, j: (layer, j, 0)),
        ],
        out_specs=pl.BlockSpec((1, tm, d), lambda bi, i, j: (bi, i, 0)),
        out_shape=jax.ShapeDtypeStruct((b, s, d), F32),
        scratch_shapes=[pltpu.VMEM((tm + 2 * FFN_HALO, d), BF16)],
        compiler_params=_params(3),
    )(x, x, x, gain, w_up, w_up, w_dw, w_dw, b_dw, b_dw, w_down)


def _rope_tables(seq_len):
    rows = seq_len // GRID_W
    row_pos = jnp.repeat(jnp.arange(rows, dtype=F32), GRID_W)
    col_pos = jnp.tile(jnp.arange(GRID_W, dtype=F32), rows)
    freqs = ROPE_THETA ** (-jnp.arange(AXIS_FREQS, dtype=F32) * 2.0 / AXIS_DIM)
    ang_r = row_pos[:, None] * freqs[None, :]
    ang_c = col_pos[:, None] * freqs[None, :]
    cr, sr, cc, sc = jnp.cos(ang_r), jnp.sin(ang_r), jnp.cos(ang_c), jnp.sin(ang_c)
    cos = jnp.concatenate([cr, cr, cc, cc], axis=-1)
    sin = jnp.concatenate([-sr, sr, -sc, sc], axis=-1)
    return cos, sin


def _qkv_kernel(x_ref, gain_ref, w_ref, qg_ref, kg_ref, cos_ref, sin_ref,
                q_ref, kt_ref, v_ref, *, scale):
    ts = x_ref.shape[1] // ROW_SUB_TILES
    qg = qg_ref[...]
    kg = kg_ref[...]
    lane = lax.broadcasted_iota(jnp.int32, (ts, HEAD_DIM), 1)
    first = (lane % (2 * AXIS_FREQS)) < AXIS_FREQS
    for c in range(ROW_SUB_TILES):
        rows = slice(c * ts, (c + 1) * ts)
        hn = _rms_norm(x_ref[0, rows, :], gain_ref[...]).astype(BF16)
        qkv = _dot(hn, w_ref[...])
        cos = cos_ref[rows, :]
        sin = sin_ref[rows, :]

        def head(col, gain):
            y = _rms_norm(qkv[:, col:col + HEAD_DIM], gain)
            partner = jnp.where(first,
                                pltpu.roll(y, HEAD_DIM - AXIS_FREQS, 1),
                                pltpu.roll(y, AXIS_FREQS, 1))
            return y * cos + partner * sin

        for h in range(N_HEADS):
            col = h * HEAD_DIM
            q_ref[0, rows, col:col + HEAD_DIM] = (head(col, qg) * scale).astype(BF16)
        for h in range(N_KV_HEADS):
            col = h * HEAD_DIM
            kt_ref[0, col:col + HEAD_DIM, rows] = head(ATTN_DIM + col, kg).T.astype(BF16)
        v_ref[0, rows, :] = qkv[:, ATTN_DIM + KV_DIM:].astype(BF16)


def _qkv(x, gain, w_qkv, q_gain, k_gain, cos, sin, layer):
    b, s, d = x.shape
    tm = _pick(s, 512)
    n = w_qkv.shape[2]
    return pl.pallas_call(
        functools.partial(_qkv_kernel, scale=HEAD_DIM ** -0.5 * LOG2_E),
        name="qkv",
        grid=(b, s // tm),
        in_specs=[
            pl.BlockSpec((1, tm, d), lambda bi, i: (bi, i, 0)),
            _resident((1, d)),
            _layer_resident((d, n), layer),
            _resident((1, HEAD_DIM)),
            _resident((1, HEAD_DIM)),
            pl.BlockSpec((tm, HEAD_DIM), lambda bi, i: (i, 0)),
            pl.BlockSpec((tm, HEAD_DIM), lambda bi, i: (i, 0)),
        ],
        out_specs=[
            pl.BlockSpec((1, tm, ATTN_DIM), lambda bi, i: (bi, i, 0)),
            pl.BlockSpec((1, KV_DIM, tm), lambda bi, i: (bi, 0, i)),
            pl.BlockSpec((1, tm, KV_DIM), lambda bi, i: (bi, i, 0)),
        ],
        out_shape=[
            jax.ShapeDtypeStruct((b, s, ATTN_DIM), BF16),
            jax.ShapeDtypeStruct((b, KV_DIM, s), BF16),
            jax.ShapeDtypeStruct((b, s, KV_DIM), BF16),
        ],
        compiler_params=_params(2),
    )(x, gain.reshape(1, d), w_qkv, q_gain.reshape(1, HEAD_DIM), k_gain.reshape(1, HEAD_DIM), cos, sin)


def _attn_kernel(q_ref, kt_ref, v_ref, o_ref, *, tq, tk, n_chunks):
    q = jnp.concatenate(
        [q_ref[0, :, g * HEAD_DIM:(g + 1) * HEAD_DIM] for g in range(GQA_GROUP)], axis=0)
    m = l = acc = None
    for c in range(n_chunks):
        s = _dot(q, kt_ref[0, :, c * tk:(c + 1) * tk])
        m_c = jnp.max(s, axis=-1, keepdims=True)
        if c == 0:
            m_new = m_c
            p = jnp.exp2(s - m_new)
            l = jnp.sum(p, axis=-1, keepdims=True)
            acc = _dot(p.astype(BF16), v_ref[0, c * tk:(c + 1) * tk, :])
        else:
            m_new = jnp.maximum(m, m_c)
            alpha = jnp.exp2(m - m_new)
            p = jnp.exp2(s - m_new)
            l = alpha * l + jnp.sum(p, axis=-1, keepdims=True)
            acc = alpha * acc + _dot(p.astype(BF16), v_ref[0, c * tk:(c + 1) * tk, :])
        m = m_new
    out = acc / l
    for g in range(GQA_GROUP):
        o_ref[0, :, g * HEAD_DIM:(g + 1) * HEAD_DIM] = out[g * tq:(g + 1) * tq].astype(BF16)


def _attention(q, kt, v):
    b, s, _ = q.shape
    tq = _pick(s, ATTN_TQ)
    tk = _pick(s, min(ATTN_TK, max(s // ATTN_MIN_CHUNKS, V7X_LANES)))
    gw = GQA_GROUP * HEAD_DIM
    return pl.pallas_call(
        functools.partial(_attn_kernel, tq=tq, tk=tk, n_chunks=s // tk),
        name="attn",
        grid=(b, N_KV_HEADS, s // tq),
        in_specs=[
            pl.BlockSpec((1, tq, gw), lambda bi, h, i: (bi, i, h)),
            pl.BlockSpec((1, HEAD_DIM, s), lambda bi, h, i: (bi, h, 0)),
            pl.BlockSpec((1, s, HEAD_DIM), lambda bi, h, i: (bi, 0, h)),
        ],
        out_specs=pl.BlockSpec((1, tq, gw), lambda bi, h, i: (bi, i, h)),
        out_shape=jax.ShapeDtypeStruct((b, s, ATTN_DIM), BF16),
        compiler_params=_params(3),
    )(q, kt, v)


def _oproj_kernel(a_ref, x_ref, w_ref, o_ref):
    o_ref[0] = x_ref[0] + _dot(a_ref[0], w_ref[...])


def _oproj(a, x, w_o, layer):
    b, s, d = x.shape
    tm = _pick(s, 512)
    return pl.pallas_call(
        _oproj_kernel,
        name="oproj",
        grid=(b, s // tm),
        in_specs=[
            pl.BlockSpec((1, tm, ATTN_DIM), lambda bi, i: (bi, i, 0)),
            pl.BlockSpec((1, tm, d), lambda bi, i: (bi, i, 0)),
            _layer_resident((ATTN_DIM, d), layer),
        ],
        out_specs=pl.BlockSpec((1, tm, d), lambda bi, i: (bi, i, 0)),
        out_shape=jax.ShapeDtypeStruct((b, s, d), F32),
        compiler_params=_params(2),
    )(a, x, w_o)


def _pw1_kernel(x_ref, gain_ref, w_ref, b_ref, u_ref, *, d):
    ts = x_ref.shape[1] // ROW_SUB_TILES
    for c in range(ROW_SUB_TILES):
        rows = slice(c * ts, (c + 1) * ts)
        hn = _rms_norm(x_ref[0, rows, :], gain_ref[...]).astype(BF16)
        u = _dot(hn, w_ref[...]) + b_ref[...]
        u_ref[0, rows, :] = u[:, :d] * _sigmoid(u[:, d:])


def _pw1(x, gain, w_pw1, b_pw1, layer):
    b, s, d = x.shape
    tm = _pick(s, 512)
    return pl.pallas_call(
        functools.partial(_pw1_kernel, d=d),
        name="pw1",
        grid=(b, s // tm),
        in_specs=[
            pl.BlockSpec((1, tm, d), lambda bi, i: (bi, i, 0)),
            _resident((1, d)),
            _layer_resident((d, 2 * d), layer),
            _resident((1, 2 * d)),
        ],
        out_specs=pl.BlockSpec((1, tm, d), lambda bi, i: (bi, i, 0)),
        out_shape=jax.ShapeDtypeStruct((b, s, d), F32),
        compiler_params=_params(2),
    )(x, gain.reshape(1, d), w_pw1, b_pw1.reshape(1, 2 * d))


def _convpw2_kernel(up_ref, u_ref, un_ref, x_ref, wdw_ref, bdw_ref, lng_ref, lnb_ref,
                    w2_ref, b2_ref, o_ref, ext_ref, conv_ref, *, tm, cw, n_seq_tiles):
    i = pl.program_id(1)
    rows = tm + 2 * CONV_HALO
    d = conv_ref.shape[1]
    ext_ref[0:CONV_HALO, :] = jnp.where(i > 0, up_ref[0], 0.0)
    ext_ref[CONV_HALO:CONV_HALO + tm, :] = u_ref[0]
    ext_ref[CONV_HALO + tm:rows, :] = jnp.where(i < n_seq_tiles - 1, un_ref[0], 0.0)

    base = CONV_HALO - CONV_K // 2
    for c in range(d // cw):
        cols = slice(c * cw, (c + 1) * cw)
        ue = ext_ref[:, cols]
        acc = None
        for r in range(V7X_SUBLANES):
            ur = ue if r == 0 else pltpu.roll(ue, rows - r, 0)
            for a in range(2 * CONV_HALO // V7X_SUBLANES):
                k = a * V7X_SUBLANES + r - base
                if 0 <= k < CONV_K:
                    term = ur[a * V7X_SUBLANES:a * V7X_SUBLANES + tm] * wdw_ref[k:k + 1, cols]
                    acc = term if acc is None else acc + term
        conv_ref[:, cols] = acc + bdw_ref[:, cols]

    y = conv_ref[...]
    mu = jnp.mean(y, axis=-1, keepdims=True)
    yc = y - mu
    var = jnp.mean(yc * yc, axis=-1, keepdims=True)
    y = yc * lax.rsqrt(var + EPS) * lng_ref[...] + lnb_ref[...]
    y = y * _sigmoid(y)
    o_ref[0] = x_ref[0] + (_dot(y.astype(BF16), w2_ref[...]) + b2_ref[...])


def _convpw2(u, x, w_dw, b_dw, ln_g, ln_b, w_pw2, b_pw2, layer):
    b, s, d = x.shape
    tm = _pick(s, 512)
    cw = _pick(d, 512)
    n_seq = s // tm
    hb = tm // CONV_HALO
    last_hb = s // CONV_HALO - 1
    row = lambda a: a.reshape(1, d)
    return pl.pallas_call(
        functools.partial(_convpw2_kernel, tm=tm, cw=cw, n_seq_tiles=n_seq),
        name="convpw2",
        grid=(b, n_seq),
        in_specs=[
            pl.BlockSpec((1, CONV_HALO, d), lambda bi, i: (bi, jnp.maximum(i * hb - 1, 0), 0)),
            pl.BlockSpec((1, tm, d), lambda bi, i: (bi, i, 0)),
            pl.BlockSpec((1, CONV_HALO, d), lambda bi, i: (bi, jnp.minimum((i + 1) * hb, last_hb), 0)),
            pl.BlockSpec((1, tm, d), lambda bi, i: (bi, i, 0)),
            _resident((CONV_K, d)),
            _resident((1, d)),
            _resident((1, d)),
            _resident((1, d)),
            _layer_resident((d, d), layer),
            _resident((1, d)),
        ],
        out_specs=pl.BlockSpec((1, tm, d), lambda bi, i: (bi, i, 0)),
        out_shape=jax.ShapeDtypeStruct((b, s, d), F32),
        scratch_shapes=[pltpu.VMEM((tm + 2 * CONV_HALO, d), F32), pltpu.VMEM((tm, d), F32)],
        compiler_params=_params(2),
    )(u, u, u, x, w_dw, row(b_dw), row(ln_g), row(ln_b), w_pw2, row(b_pw2))


def _trunk(x, p):
    depth = p["ffn_norm"].shape[0]
    cos, sin = _rope_tables(x.shape[1])
    for i in range(depth):
        j = i // 2
        if i % 2 == 0:
            q, kt, v = _qkv(x, p["attn_norm"][j], p["attn_w_qkv"], p["attn_q_norm"][j],
                            p["attn_k_norm"][j], cos, sin, j)
            x = _oproj(_attention(q, kt, v), x, p["attn_w_o"], j)
        else:
            u = _pw1(x, p["conv_norm"][j], p["conv_w_pw1"], p["conv_b_pw1"][j], j)
            x = _convpw2(u, x, p["conv_w_dw"][j], p["conv_b_dw"][j], p["conv_ln_g"][j],
                         p["conv_ln_b"][j], p["conv_w_pw2"], p["conv_b_pw2"][j], j)
        x = _ffn(x, p["ffn_norm"][i], p["ffn_w_up"], p["ffn_w_dw"][i], p["ffn_b_dw"][i],
                 p["ffn_w_down"], i)
    return x


def kernel(x_prompt, x_sample, attn_norm, attn_w_qkv, attn_q_norm, attn_k_norm, attn_w_o,
           conv_norm, conv_w_pw1, conv_b_pw1, conv_w_dw, conv_b_dw, conv_ln_g, conv_ln_b,
           conv_w_pw2, conv_b_pw2, ffn_norm, ffn_w_up, ffn_w_dw, ffn_b_dw, ffn_w_down):
    p = dict(
        attn_norm=attn_norm, attn_w_qkv=attn_w_qkv.astype(BF16), attn_q_norm=attn_q_norm,
        attn_k_norm=attn_k_norm, attn_w_o=attn_w_o.astype(BF16),
        conv_norm=conv_norm, conv_w_pw1=conv_w_pw1.astype(BF16), conv_b_pw1=conv_b_pw1,
        conv_w_dw=conv_w_dw, conv_b_dw=conv_b_dw, conv_ln_g=conv_ln_g, conv_ln_b=conv_ln_b,
        conv_w_pw2=conv_w_pw2.astype(BF16), conv_b_pw2=conv_b_pw2,
        ffn_norm=ffn_norm, ffn_w_up=_tile_up_weights(ffn_w_up.astype(BF16)), ffn_w_dw=ffn_w_dw, ffn_b_dw=ffn_b_dw,
        ffn_w_down=ffn_w_down.astype(BF16))
    return _trunk(x_prompt, p), _trunk(x_sample, p)
```

```python
import functools

import jax
import jax.numpy as jnp
from jax import lax
from jax.experimental import pallas as pl
from jax.experimental.pallas import tpu as pltpu

N_HEADS = 16
N_KV_HEADS = 4
HEAD_DIM = 128
GQA_GROUP = N_HEADS // N_KV_HEADS
ATTN_DIM = N_HEADS * HEAD_DIM
KV_DIM = N_KV_HEADS * HEAD_DIM
ROPE_THETA = 10000.0
AXIS_DIM = HEAD_DIM // 2
AXIS_FREQS = AXIS_DIM // 2
GRID_W = 64
CONV_K = 31
FFN_CONV_K = 3
EPS = 1e-6

V7X_SUBLANES = 8
V7X_LANES = 128
V7X_VMEM_LIMIT_BYTES = 56 * 1024 * 1024

FFN_HALO = V7X_SUBLANES
CONV_HALO = 2 * V7X_SUBLANES

ATTN_TQ = 256
ATTN_TK = 2048
ATTN_MIN_CHUNKS = 4
ROW_SUB_TILES = 2
LOG2_E = 1.4426950408889634

BF16 = jnp.bfloat16
F32 = jnp.float32


def _pick(total, preferred):
    if total <= preferred:
        return total
    t = preferred
    while t >= V7X_LANES:
        if total % t == 0:
            return t
        t -= V7X_LANES
    return total


def _params(n_grid):
    return pltpu.CompilerParams(
        dimension_semantics=("arbitrary",) * n_grid,
        vmem_limit_bytes=V7X_VMEM_LIMIT_BYTES)


def _resident(shape):
    return pl.BlockSpec(shape, lambda *_: (0,) * len(shape), pipeline_mode=pl.Buffered(1))


def _layer_resident(shape, layer):
    return pl.BlockSpec((None,) + shape, lambda *_: (layer,) + (0,) * len(shape),
                        pipeline_mode=pl.Buffered(1))


def _rms_norm(x, gain):
    y = x * lax.rsqrt(jnp.mean(x * x, axis=-1, keepdims=True) + EPS)
    return y * gain


def _sigmoid(x):
    return 0.5 * jnp.tanh(0.5 * x) + 0.5


def _dot(a, b):
    return jnp.dot(a, b, preferred_element_type=F32)


def _ffn_kernel(xp_ref, x_ref, xn_ref, gain_ref, wg_ref, wv_ref, dwg_ref, dwv_ref,
                bg_ref, bv_ref, wd_ref, o_ref, hn_ref, *, tm, n_seq_tiles):
    i = pl.program_id(1)
    j = pl.program_id(2)
    rows = tm + 2 * FFN_HALO

    @pl.when(j == 0)
    def _():
        gain = gain_ref[...]
        x = x_ref[0]
        hn_ref[FFN_HALO:FFN_HALO + tm, :] = _rms_norm(x, gain).astype(BF16)
        prev = jnp.where(i > 0, _rms_norm(xp_ref[0], gain), 0.0)
        hn_ref[0:FFN_HALO, :] = prev.astype(BF16)
        nxt = jnp.where(i < n_seq_tiles - 1, _rms_norm(xn_ref[0], gain), 0.0)
        hn_ref[FFN_HALO + tm:rows, :] = nxt.astype(BF16)
        o_ref[0] = x

    hn = hn_ref[...]

    def conv(w_ref, dw_ref, b_ref):
        u = _dot(hn, w_ref[...])
        dw = dw_ref[...]
        lo = pltpu.roll(u, 1, 0)
        hi = pltpu.roll(u, rows - 1, 0)
        s = slice(FFN_HALO, FFN_HALO + tm)
        return lo[s] * dw[0:1] + u[s] * dw[1:2] + hi[s] * dw[2:3] + b_ref[...]

    g = conv(wg_ref, dwg_ref, bg_ref)
    v = conv(wv_ref, dwv_ref, bv_ref)
    act = (g * _sigmoid(g)) * v
    o_ref[0] += _dot(act.astype(BF16), wd_ref[...])


def _ffn(x, gain, w_up, w_dw, b_dw, w_down, layer):
    b, s, d = x.shape
    f = w_down.shape[1]
    tm = _pick(s, 512)
    tf = _pick(f, 512)
    n_seq, n_f = s // tm, f // tf
    hb = tm // FFN_HALO
    last_hb = s // FFN_HALO - 1
    gain = gain.reshape(1, d)
    b_dw = b_dw.reshape(1, 2 * f)

    return pl.pallas_call(
        functools.partial(_ffn_kernel, tm=tm, n_seq_tiles=n_seq),
        name="ffn",
        grid=(b, n_seq, n_f),
        in_specs=[
            pl.BlockSpec((1, FFN_HALO, d), lambda bi, i, j: (bi, jnp.maximum(i * hb - 1, 0), 0)),
            pl.BlockSpec((1, tm, d), lambda bi, i, j: (bi, i, 0)),
            pl.BlockSpec((1, FFN_HALO, d), lambda bi, i, j: (bi, jnp.minimum((i + 1) * hb, last_hb), 0)),
            pl.BlockSpec((1, d), lambda bi, i, j: (0, 0)),
            pl.BlockSpec((None, d, tf), lambda bi, i, j: (layer, 0, j)),
            pl.BlockSpec((None, d, tf), lambda bi, i, j: (layer, 0, n_f + j)),
            pl.BlockSpec((FFN_CONV_K, tf), lambda bi, i, j: (0, j)),
            pl.BlockSpec((FFN_CONV_K, tf), lambda bi, i, j: (0, n_f + j)),
            pl.BlockSpec((1, tf), lambda bi, i, j: (0, j)),
            pl.BlockSpec((1, tf), lambda bi, i, j: (0, n_f + j)),
            pl.BlockSpec((None, tf, d), lambda bi, i, j: (layer, j, 0)),
        ],
        out_specs=pl.BlockSpec((1, tm, d), lambda bi, i, j: (bi, i, 0)),
        out_shape=jax.ShapeDtypeStruct((b, s, d), F32),
        scratch_shapes=[pltpu.VMEM((tm + 2 * FFN_HALO, d), BF16)],
        compiler_params=_params(3),
    )(x, x, x, gain, w_up, w_up, w_dw, w_dw, b_dw, b_dw, w_down)


def _rope_tables(seq_len):
    rows = seq_len // GRID_W
    row_pos = jnp.repeat(jnp.arange(rows, dtype=F32), GRID_W)
    col_pos = jnp.tile(jnp.arange(GRID_W, dtype=F32), rows)
    freqs = ROPE_THETA ** (-jnp.arange(AXIS_FREQS, dtype=F32) * 2.0 / AXIS_DIM)
    ang_r = row_pos[:, None] * freqs[None, :]
    ang_c = col_pos[:, None] * freqs[None, :]
    cr, sr, cc, sc = jnp.cos(ang_r), jnp.sin(ang_r), jnp.cos(ang_c), jnp.sin(ang_c)
    cos = jnp.concatenate([cr, cr, cc, cc], axis=-1)
    sin = jnp.concatenate([-sr, sr, -sc, sc], axis=-1)
    return cos, sin


def _qkv_kernel(x_ref, gain_ref, w_ref, qg_ref, kg_ref, cos_ref, sin_ref,
                q_ref, kt_ref, v_ref, *, scale):
    ts = x_ref.shape[1] // ROW_SUB_TILES
    qg = qg_ref[...]
    kg = kg_ref[...]
    lane = lax.broadcasted_iota(jnp.int32, (ts, HEAD_DIM), 1)
    first = (lane % (2 * AXIS_FREQS)) < AXIS_FREQS
    for c in range(ROW_SUB_TILES):
        rows = slice(c * ts, (c + 1) * ts)
        hn = _rms_norm(x_ref[0, rows, :], gain_ref[...]).astype(BF16)
        qkv = _dot(hn, w_ref[...])
        cos = cos_ref[rows, :]
        sin = sin_ref[rows, :]

        def head(col, gain):
            y = _rms_norm(qkv[:, col:col + HEAD_DIM], gain)
            partner = jnp.where(first,
                                pltpu.roll(y, HEAD_DIM - AXIS_FREQS, 1),
                                pltpu.roll(y, AXIS_FREQS, 1))
            return y * cos + partner * sin

        for h in range(N_HEADS):
            col = h * HEAD_DIM
            q_ref[0, rows, col:col + HEAD_DIM] = (head(col, qg) * scale).astype(BF16)
        for h in range(N_KV_HEADS):
            col = h * HEAD_DIM
            kt_ref[0, col:col + HEAD_DIM, rows] = head(ATTN_DIM + col, kg).T.astype(BF16)
        v_ref[0, rows, :] = qkv[:, ATTN_DIM + KV_DIM:].astype(BF16)


def _qkv(x, gain, w_qkv, q_gain, k_gain, cos, sin, layer):
    b, s, d = x.shape
    tm = _pick(s, 512)
    n = w_qkv.shape[2]
    return pl.pallas_call(
        functools.partial(_qkv_kernel, scale=HEAD_DIM ** -0.5 * LOG2_E),
        name="qkv",
        grid=(b, s // tm),
        in_specs=[
            pl.BlockSpec((1, tm, d), lambda bi, i: (bi, i, 0)),
            _resident((1, d)),
            _layer_resident((d, n), layer),
            _resident((1, HEAD_DIM)),
            _resident((1, HEAD_DIM)),
            pl.BlockSpec((tm, HEAD_DIM), lambda bi, i: (i, 0)),
            pl.BlockSpec((tm, HEAD_DIM), lambda bi, i: (i, 0)),
        ],
        out_specs=[
            pl.BlockSpec((1, tm, ATTN_DIM), lambda bi, i: (bi, i, 0)),
            pl.BlockSpec((1, KV_DIM, tm), lambda bi, i: (bi, 0, i)),
            pl.BlockSpec((1, tm, KV_DIM), lambda bi, i: (bi, i, 0)),
        ],
        out_shape=[
            jax.ShapeDtypeStruct((b, s, ATTN_DIM), BF16),
            jax.ShapeDtypeStruct((b, KV_DIM, s), BF16),
            jax.ShapeDtypeStruct((b, s, KV_DIM), BF16),
        ],
        compiler_params=_params(2),
    )(x, gain.reshape(1, d), w_qkv, q_gain.reshape(1, HEAD_DIM), k_gain.reshape(1, HEAD_DIM), cos, sin)


def _attn_kernel(q_ref, kt_ref, v_ref, o_ref, *, tq, tk, n_chunks):
    q = jnp.concatenate(
        [q_ref[0, :, g * HEAD_DIM:(g + 1) * HEAD_DIM] for g in range(GQA_GROUP)], axis=0)
    m = l = acc = None
    for c in range(n_chunks):
        s = _dot(q, kt_ref[0, :, c * tk:(c + 1) * tk])
        m_c = jnp.max(s, axis=-1, keepdims=True)
        if c == 0:
            m_new = m_c
            p = jnp.exp2(s - m_new)
            l = jnp.sum(p, axis=-1, keepdims=True)
            acc = _dot(p.astype(BF16), v_ref[0, c * tk:(c + 1) * tk, :])
        else:
            m_new = jnp.maximum(m, m_c)
            alpha = jnp.exp2(m - m_new)
            p = jnp.exp2(s - m_new)
            l = alpha * l + jnp.sum(p, axis=-1, keepdims=True)
            acc = alpha * acc + _dot(p.astype(BF16), v_ref[0, c * tk:(c + 1) * tk, :])
        m = m_new
    out = acc / l
    for g in range(GQA_GROUP):
        o_ref[0, :, g * HEAD_DIM:(g + 1) * HEAD_DIM] = out[g * tq:(g + 1) * tq].astype(BF16)


def _attention(q, kt, v):
    b, s, _ = q.shape
    tq = _pick(s, ATTN_TQ)
    tk = _pick(s, min(ATTN_TK, max(s // ATTN_MIN_CHUNKS, V7X_LANES)))
    gw = GQA_GROUP * HEAD_DIM
    return pl.pallas_call(
        functools.partial(_attn_kernel, tq=tq, tk=tk, n_chunks=s // tk),
        name="attn",
        grid=(b, N_KV_HEADS, s // tq),
        in_specs=[
            pl.BlockSpec((1, tq, gw), lambda bi, h, i: (bi, i, h)),
            pl.BlockSpec((1, HEAD_DIM, s), lambda bi, h, i: (bi, h, 0)),
            pl.BlockSpec((1, s, HEAD_DIM), lambda bi, h, i: (bi, 0, h)),
        ],
        out_specs=pl.BlockSpec((1, tq, gw), lambda bi, h, i: (bi, i, h)),
        out_shape=jax.ShapeDtypeStruct((b, s, ATTN_DIM), BF16),
        compiler_params=_params(3),
    )(q, kt, v)


def _oproj_kernel(a_ref, x_ref, w_ref, o_ref):
    o_ref[0] = x_ref[0] + _dot(a_ref[0], w_ref[...])


def _oproj(a, x, w_o, layer):
    b, s, d = x.shape
    tm = _pick(s, 512)
    return pl.pallas_call(
        _oproj_kernel,
        name="oproj",
        grid=(b, s // tm),
        in_specs=[
            pl.BlockSpec((1, tm, ATTN_DIM), lambda bi, i: (bi, i, 0)),
            pl.BlockSpec((1, tm, d), lambda bi, i: (bi, i, 0)),
            _layer_resident((ATTN_DIM, d), layer),
        ],
        out_specs=pl.BlockSpec((1, tm, d), lambda bi, i: (bi, i, 0)),
        out_shape=jax.ShapeDtypeStruct((b, s, d), F32),
        compiler_params=_params(2),
    )(a, x, w_o)


def _pw1_kernel(x_ref, gain_ref, w_ref, b_ref, u_ref, *, d):
    ts = x_ref.shape[1] // ROW_SUB_TILES
    for c in range(ROW_SUB_TILES):
        rows = slice(c * ts, (c + 1) * ts)
        hn = _rms_norm(x_ref[0, rows, :], gain_ref[...]).astype(BF16)
        u = _dot(hn, w_ref[...]) + b_ref[...]
        u_ref[0, rows, :] = u[:, :d] * _sigmoid(u[:, d:])


def _pw1(x, gain, w_pw1, b_pw1, layer):
    b, s, d = x.shape
    tm = _pick(s, 512)
    return pl.pallas_call(
        functools.partial(_pw1_kernel, d=d),
        name="pw1",
        grid=(b, s // tm),
        in_specs=[
            pl.BlockSpec((1, tm, d), lambda bi, i: (bi, i, 0)),
            _resident((1, d)),
            _layer_resident((d, 2 * d), layer),
            _resident((1, 2 * d)),
        ],
        out_specs=pl.BlockSpec((1, tm, d), lambda bi, i: (bi, i, 0)),
        out_shape=jax.ShapeDtypeStruct((b, s, d), F32),
        compiler_params=_params(2),
    )(x, gain.reshape(1, d), w_pw1, b_pw1.reshape(1, 2 * d))


def _convpw2_kernel(up_ref, u_ref, un_ref, x_ref, wdw_ref, bdw_ref, lng_ref, lnb_ref,
                    w2_ref, b2_ref, o_ref, ext_ref, conv_ref, *, tm, cw, n_seq_tiles):
    i = pl.program_id(1)
    rows = tm + 2 * CONV_HALO
    d = conv_ref.shape[1]
    ext_ref[0:CONV_HALO, :] = jnp.where(i > 0, up_ref[0], 0.0)
    ext_ref[CONV_HALO:CONV_HALO + tm, :] = u_ref[0]
    ext_ref[CONV_HALO + tm:rows, :] = jnp.where(i < n_seq_tiles - 1, un_ref[0], 0.0)

    base = CONV_HALO - CONV_K // 2
    for c in range(d // cw):
        cols = slice(c * cw, (c + 1) * cw)
        ue = ext_ref[:, cols]
        acc = None
        for r in range(V7X_SUBLANES):
            ur = ue if r == 0 else pltpu.roll(ue, rows - r, 0)
            for a in range(2 * CONV_HALO // V7X_SUBLANES):
                k = a * V7X_SUBLANES + r - base
                if 0 <= k < CONV_K:
                    term = ur[a * V7X_SUBLANES:a * V7X_SUBLANES + tm] * wdw_ref[k:k + 1, cols]
                    acc = term if acc is None else acc + term
        conv_ref[:, cols] = acc + bdw_ref[:, cols]

    y = conv_ref[...]
    mu = jnp.mean(y, axis=-1, keepdims=True)
    yc = y - mu
    var = jnp.mean(yc * yc, axis=-1, keepdims=True)
    y = yc * lax.rsqrt(var + EPS) * lng_ref[...] + lnb_ref[...]
    y = y * _sigmoid(y)
    o_ref[0] = x_ref[0] + (_dot(y.astype(BF16), w2_ref[...]) + b2_ref[...])


def _convpw2(u, x, w_dw, b_dw, ln_g, ln_b, w_pw2, b_pw2, layer):
    b, s, d = x.shape
    tm = _pick(s, 512)
    cw = _pick(d, 512)
    n_seq = s // tm
    hb = tm // CONV_HALO
    last_hb = s // CONV_HALO - 1
    row = lambda a: a.reshape(1, d)
    return pl.pallas_call(
        functools.partial(_convpw2_kernel, tm=tm, cw=cw, n_seq_tiles=n_seq),
        name="convpw2",
        grid=(b, n_seq),
        in_specs=[
            pl.BlockSpec((1, CONV_HALO, d), lambda bi, i: (bi, jnp.maximum(i * hb - 1, 0), 0)),
            pl.BlockSpec((1, tm, d), lambda bi, i: (bi, i, 0)),
            pl.BlockSpec((1, CONV_HALO, d), lambda bi, i: (bi, jnp.minimum((i + 1) * hb, last_hb), 0)),
            pl.BlockSpec((1, tm, d), lambda bi, i: (bi, i, 0)),
            _resident((CONV_K, d)),
            _resident((1, d)),
            _resident((1, d)),
            _resident((1, d)),
            _layer_resident((d, d), layer),
            _resident((1, d)),
        ],
        out_specs=pl.BlockSpec((1, tm, d), lambda bi, i: (bi, i, 0)),
        out_shape=jax.ShapeDtypeStruct((b, s, d), F32),
        scratch_shapes=[pltpu.VMEM((tm + 2 * CONV_HALO, d), F32), pltpu.VMEM((tm, d), F32)],
        compiler_params=_params(2),
    )(u, u, u, x, w_dw, row(b_dw), row(ln_g), row(ln_b), w_pw2, row(b_pw2))


def _trunk(x, p):
    depth = p["ffn_norm"].shape[0]
    cos, sin = _rope_tables(x.shape[1])
    for i in range(depth):
        j = i // 2
        if i % 2 == 0:
            q, kt, v = _qkv(x, p["attn_norm"][j], p["attn_w_qkv"], p["attn_q_norm"][j],
                            p["attn_k_norm"][j], cos, sin, j)
            x = _oproj(_attention(q, kt, v), x, p["attn_w_o"], j)
        else:
            u = _pw1(x, p["conv_norm"][j], p["conv_w_pw1"], p["conv_b_pw1"][j], j)
            x = _convpw2(u, x, p["conv_w_dw"][j], p["conv_b_dw"][j], p["conv_ln_g"][j],
                         p["conv_ln_b"][j], p["conv_w_pw2"], p["conv_b_pw2"][j], j)
        x = _ffn(x, p["ffn_norm"][i], p["ffn_w_up"], p["ffn_w_dw"][i], p["ffn_b_dw"][i],
                 p["ffn_w_down"], i)
    return x


def kernel(x_prompt, x_sample, attn_norm, attn_w_qkv, attn_q_norm, attn_k_norm, attn_w_o,
           conv_norm, conv_w_pw1, conv_b_pw1, conv_w_dw, conv_b_dw, conv_ln_g, conv_ln_b,
           conv_w_pw2, conv_b_pw2, ffn_norm, ffn_w_up, ffn_w_dw, ffn_b_dw, ffn_w_down):
    p = dict(
        attn_norm=attn_norm, attn_w_qkv=attn_w_qkv.astype(BF16), attn_q_norm=attn_q_norm,
        attn_k_norm=attn_k_norm, attn_w_o=attn_w_o.astype(BF16),
        conv_norm=conv_norm, conv_w_pw1=conv_w_pw1.astype(BF16), conv_b_pw1=conv_b_pw1,
        conv_w_dw=conv_w_dw, conv_b_dw=conv_b_dw, conv_ln_g=conv_ln_g, conv_ln_b=conv_ln_b,
        conv_w_pw2=conv_w_pw2.astype(BF16), conv_b_pw2=conv_b_pw2,
        ffn_norm=ffn_norm, ffn_w_up=ffn_w_up.astype(BF16), ffn_w_dw=ffn_w_dw, ffn_b_dw=ffn_b_dw,
        ffn_w_down=ffn_w_down.astype(BF16))
    return _trunk(x_prompt, p), _trunk(x_sample, p)
```
